```python
import math
import jax
import jax.numpy as jnp
from jax import lax
import numpy as np

D_MODEL = 4096
BATCH = 4
SEQ = 4096
DEPTH = 1

POOL_WINDOWS = (2, 4, 8, 16)
N_POOL_GROUPS = len(POOL_WINDOWS)
POOL_GROUP = D_MODEL // 8
POOL_WIDTH = POOL_GROUP * N_POOL_GROUPS
HEAD_DIM = 128
N_HEADS = D_MODEL // 256
KEY_WIDTH = N_HEADS * HEAD_DIM
VALUE_WIDTH = N_HEADS * HEAD_DIM
CONV_WIDTH = 4
CONV_CHANNELS = 2 * KEY_WIDTH + VALUE_WIDTH
CHUNK = 64
N_BRANCHES = 2
D_FF = 4 * D_MODEL
N_MOD = 6
EPS = 1e-6

IN_SPLITS = (
    POOL_WIDTH,
    POOL_WIDTH + CONV_CHANNELS,
    POOL_WIDTH + CONV_CHANNELS + VALUE_WIDTH,
    POOL_WIDTH + CONV_CHANNELS + VALUE_WIDTH + N_HEADS,
    POOL_WIDTH + CONV_CHANNELS + VALUE_WIDTH + 2 * N_HEADS,
)
IN_WIDTH = IN_SPLITS[-1] + N_BRANCHES * D_MODEL

kernel_name = "hybrid_pool_deltanet_adaln_block"


def rms_norm(x, g):
    xf = x.astype(jnp.float32)
    y = xf * lax.rsqrt(jnp.mean(xf * xf, axis=-1, keepdims=True) + EPS) * g.astype(jnp.float32)
    return y.astype(x.dtype)


def modulate(n, shift, scale):
    return n * (1 + scale) + shift


def l2_normalize(t):
    return t * lax.rsqrt(jnp.sum(t * t, axis=-1, keepdims=True) + EPS)


def causal_multiscale_pool(p, pool_w, pool_scale):
    B, S, _ = p.shape
    pf = p.astype(jnp.float32).reshape(B, S, N_POOL_GROUPS, POOL_GROUP)
    cs = jnp.pad(jnp.cumsum(pf, axis=1), ((0, 0), (1, 0), (0, 0), (0, 0)))
    pos = jnp.arange(1, S + 1, dtype=jnp.float32)
    outs = []
    for gi, w in enumerate(POOL_WINDOWS):
        c_g = cs[:, :, gi]
        lower = jnp.pad(c_g[:, : S + 1 - w], ((0, 0), (w, 0), (0, 0)))
        count = jnp.minimum(pos, float(w))[None, :, None]
        outs.append((c_g[:, 1:] - lower[:, 1:]) / count - pf[:, :, gi])
    mixed = jnp.stack(outs, axis=2).astype(p.dtype)
    mixed = jnp.einsum("bsgc,gcd->bsgd", mixed, pool_w)
    return mixed.reshape(B, S, POOL_WIDTH) * pool_scale


def causal_depthwise_conv(t, conv_w):
    C = t.shape[-1]
    return lax.conv_general_dilated(
        t, conv_w.reshape(CONV_WIDTH, 1, C).astype(t.dtype),
        window_strides=(1,), padding=[(CONV_WIDTH - 1, 0)],
        dimension_numbers=("NWC", "WIO", "NWC"), feature_group_count=C)


def chunk_gated_delta_rule(q, k, v, log_decay, beta):
    B, S, H, dk = q.shape
    dv = v.shape[-1]
    N = S // CHUNK

    def to_chunks(t):
        t = t.reshape((B, N, CHUNK, H) + t.shape[3:])
        return jnp.moveaxis(t, 3, 2)

    q, k, v = to_chunks(q), to_chunks(k), to_chunks(v)
    g = jnp.cumsum(to_chunks(log_decay), axis=-1)
    beta = to_chunks(beta)

    causal = jnp.tril(jnp.ones((CHUNK, CHUNK), dtype=bool))
    strict = jnp.tril(jnp.ones((CHUNK, CHUNK), dtype=bool), k=-1)
    diff = g[..., :, None] - g[..., None, :]
    decay = jnp.where(causal, jnp.exp(jnp.where(causal, diff, 0.0)), 0.0)

    k_beta = k * beta[..., None]
    L = jnp.where(strict, jnp.einsum("bnhid,bnhjd->bnhij", k_beta, k) * decay, 0.0)
    A = L + jnp.eye(CHUNK, dtype=L.dtype)
    rhs = jnp.concatenate([v * beta[..., None], k_beta * jnp.exp(g)[..., None]], axis=-1)
    sol = lax.linalg.triangular_solve(A, rhs, left_side=True, lower=True, unit_diagonal=True)
    value, k_cumdecay = sol[..., :dv], sol[..., dv:]
    intra = jnp.where(causal, jnp.einsum("bnhid,bnhjd->bnhij", q, k) * decay, 0.0)

    def step(state, inp):
        q_c, k_c, val_c, kcd_c, g_c, intra_c = inp
        v_new = val_c - jnp.einsum("bhcd,bhde->bhce", kcd_c, state)
        o = (jnp.einsum("bhcd,bhde->bhce", q_c * jnp.exp(g_c)[..., None], state)
             + jnp.einsum("bhij,bhje->bhie", intra_c, v_new))
        g_last = g_c[..., -1]
        k_w = k_c * jnp.exp(g_last[..., None] - g_c)[..., None]
        state = state * jnp.exp(g_last)[..., None, None] + jnp.einsum("bhcd,bhce->bhde", k_w, v_new)
        return state, o

    xs = tuple(jnp.moveaxis(t, 1, 0) for t in (q, k, value, k_cumdecay, g, intra))
    state0 = jnp.zeros((B, H, dk, dv), jnp.float32)
    _, o = lax.scan(step, state0, xs)
    return jnp.transpose(o, (1, 0, 3, 2, 4)).reshape(B, S, H, dv)


def gated_deltanet(qkv, z, a, b, conv_w, a_log, dt_bias, head_norm_g):
    B, S, _ = qkv.shape
    qkv = jax.nn.silu(causal_depthwise_conv(qkv, conv_w))
    q, k, v = jnp.split(qkv, [KEY_WIDTH, 2 * KEY_WIDTH], axis=-1)
    q = l2_normalize(q.reshape(B, S, N_HEADS, HEAD_DIM).astype(jnp.float32)) * (HEAD_DIM ** -0.5)
    k = l2_normalize(k.reshape(B, S, N_HEADS, HEAD_DIM).astype(jnp.float32))
    v = v.reshape(B, S, N_HEADS, HEAD_DIM).astype(jnp.float32)
    log_decay = -jnp.exp(a_log.astype(jnp.float32)) * jax.nn.softplus(
        a.astype(jnp.float32) + dt_bias.astype(jnp.float32))
    beta = jax.nn.sigmoid(b.astype(jnp.float32))
    o = chunk_gated_delta_rule(q, k, v, log_decay, beta)
    o = o * lax.rsqrt(jnp.mean(o * o, axis=-1, keepdims=True) + EPS) * head_norm_g.astype(jnp.float32)
    o = o * jax.nn.silu(z.reshape(B, S, N_HEADS, HEAD_DIM).astype(jnp.float32))
    return o.reshape(B, S, VALUE_WIDTH).astype(qkv.dtype)


def hybrid_mixer(u, w_in, conv_w, pool_w, pool_scale, a_log, dt_bias, head_norm_g,
                 w_up_pool, w_up_delta, w_out):
    proj = u @ w_in
    p, qkv, z, a, b, gates = jnp.split(proj, list(IN_SPLITS), axis=-1)
    y_pool = causal_multiscale_pool(p, pool_w, pool_scale) @ w_up_pool
    y_delta = gated_deltanet(qkv, z, a, b, conv_w, a_log, dt_bias, head_norm_g) @ w_up_delta
    g_pool, g_delta = jnp.split(jax.nn.sigmoid(gates), N_BRANCHES, axis=-1)
    return (g_pool * y_pool + g_delta * y_delta) @ w_out


def squared_relu_mlp(u, w_ff_in, w_ff_out):
    hdn = jax.nn.relu(u @ w_ff_in)
    return (hdn * hdn) @ w_ff_out


def setup_inputs(seed: int = 0) -> dict:
    key = jax.random.key(seed)
    ks = jax.random.split(key, 24)
    f32 = jnp.float32

    def nrm(k, shape, scale):
        return jax.random.normal(k, shape, f32) * scale

    dt = jnp.exp(jax.random.uniform(ks[10], (N_HEADS,), f32, math.log(1e-3), math.log(1e-1)))
    return {
        "x": nrm(ks[0], (BATCH, SEQ, D_MODEL), 1.0),
        "c": nrm(ks[1], (BATCH, D_MODEL), 1.0),
        "w_ada": nrm(ks[2], (D_MODEL, N_MOD * D_MODEL), D_MODEL ** -0.5),
        "b_ada": nrm(ks[3], (N_MOD * D_MODEL,), 0.02),
        "norm1_g": 1.0 + nrm(ks[4], (D_MODEL,), 0.02),
        "w_in": nrm(ks[5], (D_MODEL, IN_WIDTH), D_MODEL ** -0.5),
        "conv_w": nrm(ks[6], (CONV_WIDTH, CONV_CHANNELS), CONV_WIDTH ** -0.5),
        "pool_w": nrm(ks[7], (N_POOL_GROUPS, POOL_GROUP, POOL_GROUP), POOL_GROUP ** -0.5),
        "pool_scale": 1.0 + nrm(ks[8], (POOL_WIDTH,), 0.02),
        "a_log": jnp.log(jax.random.uniform(ks[9], (N_HEADS,), f32, 1.0, 16.0)),
        "dt_bias": dt + jnp.log(-jnp.expm1(-dt)),
        "head_norm_g": 1.0 + nrm(ks[11], (HEAD_DIM,), 0.02),
        "w_up_pool": nrm(ks[12], (POOL_WIDTH, D_MODEL), POOL_WIDTH ** -0.5),
        "w_up_delta": nrm(ks[13], (VALUE_WIDTH, D_MODEL), VALUE_WIDTH ** -0.5),
        "w_out": nrm(ks[14], (D_MODEL, D_MODEL), D_MODEL ** -0.5),
        "norm2_g": 1.0 + nrm(ks[15], (D_MODEL,), 0.02),
        "w_ff_in": nrm(ks[16], (D_MODEL, D_FF), D_MODEL ** -0.5),
        "w_ff_out": nrm(ks[17], (D_FF, D_MODEL), D_FF ** -0.5),
        "final_g": 1.0 + nrm(ks[18], (D_MODEL,), 0.02),
    }


def reference(x, c, w_ada, b_ada, norm1_g, w_in, conv_w, pool_w, pool_scale, a_log, dt_bias,
              head_norm_g, w_up_pool, w_up_delta, w_out, norm2_g, w_ff_in, w_ff_out, final_g):
    h = x
    for _ in range(DEPTH):
        mod = (jax.nn.silu(c) @ w_ada + b_ada)[:, None, :]
        shift1, scale1, gate1, shift2, scale2, gate2 = jnp.split(mod, N_MOD, axis=-1)
        u = modulate(rms_norm(h, norm1_g), shift1, scale1)
        h = h + gate1 * hybrid_mixer(u, w_in, conv_w, pool_w, pool_scale, a_log, dt_bias,
                                     head_norm_g, w_up_pool, w_up_delta, w_out)
        u = modulate(rms_norm(h, norm2_g), shift2, scale2)
        h = h + gate2 * squared_relu_mlp(u, w_ff_in, w_ff_out)
    return rms_norm(h, final_g)
```

```python
import functools

import jax
import jax.numpy as jnp
from jax import lax
from jax.experimental import pallas as pl
from jax.experimental.pallas import tpu as pltpu

EPS = 1e-6
HEAD_DIM = 128
CHUNK = 64
CONV_WIDTH = 4
POOL_WINDOWS = (2, 4, 8, 16)
N_MOD = 6
LANES = 128
VMEM_LIMIT = 56 * 1024 * 1024
F32 = jnp.float32
BF16 = jnp.bfloat16
HIGHEST = lax.Precision.HIGHEST


def _params(sem):
    return pltpu.CompilerParams(dimension_semantics=sem, vmem_limit_bytes=VMEM_LIMIT)


def _tile(n, pref):
    t = min(n, pref)
    while n % t:
        t //= 2
    return t


def _silu(x):
    return x * jax.nn.sigmoid(x)


def _ada_kernel(c_ref, w_ref, b_ref, o_ref):
    s = _silu(c_ref[...]).astype(BF16)
    o_ref[...] = jnp.dot(s, w_ref[...].astype(BF16), preferred_element_type=F32) + b_ref[...]


def _ada(c, w_ada, b_ada):
    B, D = c.shape
    N = w_ada.shape[1]
    rows = 8
    cp = jnp.zeros((rows, D), F32).at[:B].set(c)
    tn = _tile(N, 512)
    out = pl.pallas_call(
        _ada_kernel,
        grid=(N // tn,),
        in_specs=[pl.BlockSpec((rows, D), lambda j: (0, 0)),
                  pl.BlockSpec((D, tn), lambda j: (0, j)),
                  pl.BlockSpec((1, tn), lambda j: (0, j))],
        out_specs=pl.BlockSpec((rows, tn), lambda j: (0, j)),
        out_shape=jax.ShapeDtypeStruct((rows, N), F32),
        compiler_params=_params(("parallel",)),
        name="ada",
    )(cp, w_ada, b_ada.reshape(1, N))
    return out[:B]


def _norm_kernel(x_ref, g_ref, shift_ref, scale_ref, o_ref):
    x = x_ref[...]
    n = x * lax.rsqrt(jnp.mean(x * x, axis=-1, keepdims=True) + EPS) * g_ref[...]
    o_ref[...] = (n * (1.0 + scale_ref[0]) + shift_ref[0]).astype(o_ref.dtype)


def _norm_mod(x2, g, shift, scale, seq):
    T, D = x2.shape
    tm = _tile(seq, 256)
    bmap = lambda i: ((i * tm) // seq, 0, 0)
    return pl.pallas_call(
        _norm_kernel,
        grid=(T // tm,),
        in_specs=[pl.BlockSpec((tm, D), lambda i: (i, 0)),
                  pl.BlockSpec((1, D), lambda i: (0, 0)),
                  pl.BlockSpec((1, 1, D), bmap),
                  pl.BlockSpec((1, 1, D), bmap)],
        out_specs=pl.BlockSpec((tm, D), lambda i: (i, 0)),
        out_shape=jax.ShapeDtypeStruct((T, D), BF16),
        compiler_params=_params(("parallel",)),
        name="norm_mod",
    )(x2, g.reshape(1, D), shift, scale)


def _mm_kernel(a_ref, w_ref, o_ref):
    o_ref[...] = jnp.dot(a_ref[...], w_ref[...], preferred_element_type=F32).astype(o_ref.dtype)


def _matmul(a, w, out_dtype, tm_pref=1024, tn_pref=1024, name="mm"):
    M, K = a.shape
    N = w.shape[1]
    tm, tn = _tile(M, tm_pref), _tile(N, tn_pref)
    return pl.pallas_call(
        _mm_kernel,
        grid=(M // tm, N // tn),
        in_specs=[pl.BlockSpec((tm, K), lambda i, j: (i, 0)),
                  pl.BlockSpec((K, tn), lambda i, j: (0, j))],
        out_specs=pl.BlockSpec((tm, tn), lambda i, j: (i, j)),
        out_shape=jax.ShapeDtypeStruct((M, N), out_dtype),
        compiler_params=_params(("parallel", "parallel")),
        name=name,
    )(a, w)


def _pool_kernel(p_ref, halo_ref, w_ref, scale_ref, o_ref, *, seq, tm, halo):
    i = pl.program_id(0)
    g = pl.program_id(1)
    start = (i * tm) % seq
    p = p_ref[...]
    hl = jnp.where(start > 0, halo_ref[...], jnp.zeros_like(halo_ref))
    ext = jnp.concatenate([jnp.zeros((LANES - halo, p.shape[1]), p.dtype), hl, p], axis=0)
    win = jnp.where(g == 0, POOL_WINDOWS[0],
                    jnp.where(g == 1, POOL_WINDOWS[1],
                              jnp.where(g == 2, POOL_WINDOWS[2], POOL_WINDOWS[3])))
    r = lax.broadcasted_iota(jnp.int32, (tm, tm + LANES), 0)
    cidx = lax.broadcasted_iota(jnp.int32, (tm, tm + LANES), 1)
    d = r + LANES - cidx
    band = jnp.where((d >= 0) & (d < win), 1.0, 0.0).astype(BF16)
    wsum = jnp.dot(band, ext, preferred_element_type=F32)
    pos = start + lax.broadcasted_iota(jnp.int32, (tm, 1), 0) + 1
    count = jnp.minimum(pos, win).astype(F32)
    mixed = wsum / count - p.astype(F32)
    y = jnp.dot(mixed.astype(BF16), w_ref[0], preferred_element_type=F32)
    o_ref[...] = (y * scale_ref[...]).astype(o_ref.dtype)


def _pool(proj, pool_w_bf, pool_scale, seq):
    T = proj.shape[0]
    G, Cg, _ = pool_w_bf.shape
    assert G == len(POOL_WINDOWS)
    halo = 16
    tm = _tile(seq, 512)
    hb = tm // halo
    kern = functools.partial(_pool_kernel, seq=seq, tm=tm, halo=halo)
    return pl.pallas_call(
        kern,
        grid=(T // tm, G),
        in_specs=[pl.BlockSpec((tm, Cg), lambda i, g: (i, g)),
                  pl.BlockSpec((halo, Cg), lambda i, g: (jnp.maximum(i * hb - 1, 0), g)),
                  pl.BlockSpec((1, Cg, Cg), lambda i, g: (g, 0, 0)),
                  pl.BlockSpec((1, Cg), lambda i, g: (0, g))],
        out_specs=pl.BlockSpec((tm, Cg), lambda i, g: (i, g)),
        out_shape=jax.ShapeDtypeStruct((T, G * Cg), BF16),
        compiler_params=_params(("parallel", "parallel")),
        name="pool",
    )(proj, proj, pool_w_bf, pool_scale.reshape(1, G * Cg))


SOLVE_BASE_LOG2 = 3


def _bdot(a, b):
    return jnp.dot(a.astype(BF16), b.astype(BF16), preferred_element_type=F32)


def _solve_unit_lower(L, X, ri, ci):
    b = SOLVE_BASE_LOG2
    A = jnp.where((ri >> b) == (ci >> b), L, 0.0)
    A2 = _bdot(A, A)
    A3 = _bdot(A, A2)
    A4 = _bdot(A2, A2)
    T0 = A2 - A - A3
    Tm = T0 + A4 + _bdot(T0, A4)
    lvl = b
    while (1 << lvl) < CHUNK:
        Cm = jnp.where(((ri >> (lvl + 1)) == (ci >> (lvl + 1))) & ((ri >> lvl) != (ci >> lvl)), L, 0.0)
        M = Cm + _bdot(Cm, Tm)
        Tm = Tm - M - _bdot(Tm, M)
        lvl += 1
    return X + _bdot(Tm, X)


def _delta_kernel(q_ref, k_ref, v_ref, z_ref, ab_ref, wq_ref, wk_ref, wv_ref, nega_ref, dtb_ref,
                  hng_ref, o_ref, s_scr, halo_scr, qs_scr, ks_scr, vs_scr, ld_scr, beta_scr,
                  *, hb, rows, n_heads):
    hg = pl.program_id(1)
    cb = pl.program_id(2)
    C = hb * HEAD_DIM

    @pl.when(cb == 0)
    def _():
        s_scr[...] = jnp.zeros_like(s_scr)
        halo_scr[...] = jnp.zeros_like(halo_scr)

    def conv(idx, x_ref, w_ref):
        x = x_ref[...].astype(F32)
        ext = jnp.concatenate([halo_scr[idx], x], axis=0)
        w = w_ref[...]
        y = x * w[CONV_WIDTH - 1:CONV_WIDTH]
        for s in range(1, CONV_WIDTH):
            y = y + ext[8 - s:8 - s + rows] * w[CONV_WIDTH - 1 - s:CONV_WIDTH - s]
        halo_scr[idx] = x[rows - 8:rows]
        return _silu(y)

    def l2n(t, mult):
        outs = []
        for j in range(hb):
            th = t[:, j * HEAD_DIM:(j + 1) * HEAD_DIM]
            outs.append(th * (lax.rsqrt(jnp.sum(th * th, axis=-1, keepdims=True) + EPS) * mult))
        return jnp.concatenate(outs, axis=1) if hb > 1 else outs[0]

    qs_scr[...] = l2n(conv(0, q_ref, wq_ref), HEAD_DIM ** -0.5)
    ks_scr[...] = l2n(conv(1, k_ref, wk_ref), 1.0)
    vs_scr[...] = conv(2, v_ref, wv_ref)

    ab = ab_ref[...]
    ld_scr[...] = nega_ref[...] * jax.nn.softplus(ab + dtb_ref[...])
    beta_scr[...] = jax.nn.sigmoid(ab)

    ri = lax.broadcasted_iota(jnp.int32, (CHUNK, CHUNK), 0)
    ci = lax.broadcasted_iota(jnp.int32, (CHUNK, CHUNK), 1)
    causal = ri >= ci
    strict = ri > ci
    tri = causal.astype(F32)
    r2 = lax.broadcasted_iota(jnp.int32, (CHUNK, 2 * LANES), 0)
    c2 = lax.broadcasted_iota(jnp.int32, (CHUNK, 2 * LANES), 1)
    cmask = jnp.where(((c2 < CHUNK) & (r2 > c2)) | (c2 >= LANES), 1.0, 0.0)

    lane = lax.broadcasted_iota(jnp.int32, (CHUNK, LANES), 1)

    def chunk_body(c, carry):
        r0 = pl.multiple_of(c * CHUNK, CHUNK)
        ldc = ld_scr[pl.ds(r0, CHUNK), :]
        btc = beta_scr[pl.ds(r0, CHUNK), :]
        for j in range(hb):
            h = hg * hb + j
            sl = slice(j * HEAD_DIM, (j + 1) * HEAD_DIM)
            q = qs_scr[pl.ds(r0, CHUNK), sl]
            k = ks_scr[pl.ds(r0, CHUNK), sl]
            v = vs_scr[pl.ds(r0, CHUNK), sl]
            ld_b = jnp.sum(jnp.where(lane == h, ldc, 0.0), axis=-1, keepdims=True)
            beta_b = jnp.sum(jnp.where(lane == n_heads + h, btc, 0.0), axis=-1, keepdims=True)
            ld2 = jnp.broadcast_to(ld_b, (CHUNK, 2 * LANES)) * cmask
            cs = jnp.dot(tri, ld2, preferred_element_type=F32, precision=HIGHEST)
            diff = cs[:, :CHUNK]
            gb = cs[:, LANES:]
            decay = jnp.where(causal, jnp.exp(jnp.where(causal, diff, 0.0)), 0.0)
            eg = jnp.exp(gb)
            g_last = gb[CHUNK - 1:CHUNK, :]
            ew = jnp.exp(g_last - gb)
            e_last = jnp.exp(g_last)

            kb = k * beta_b
            kq = lax.dot_general(jnp.concatenate([kb, q], axis=0).astype(BF16), k.astype(BF16),
                                 (((1,), (1,)), ((), ())), preferred_element_type=F32)
            L = jnp.where(strict, kq[:CHUNK] * decay, 0.0)
            intra = kq[CHUNK:] * decay
            X = jnp.concatenate([v * beta_b, kb * eg], axis=1)
            X = _solve_unit_lower(L, X, ri, ci)
            Xb = X.astype(BF16)
            iv = jnp.dot(intra.astype(BF16), Xb, preferred_element_type=F32)
            kv = lax.dot_general((k * ew).astype(BF16), Xb, (((0,), (0,)), ((), ())),
                                 preferred_element_type=F32)
            qe = q * eg - iv[:, HEAD_DIM:]
            S = s_scr[j]
            lhs = jnp.concatenate([kv[:, HEAD_DIM:], qe], axis=0).astype(BF16)
            rs = jnp.dot(lhs, S.astype(BF16), preferred_element_type=F32)
            s_scr[j] = e_last * S - rs[:HEAD_DIM] + kv[:, :HEAD_DIM]
            o = rs[HEAD_DIM:] + iv[:, :HEAD_DIM]
            o = o * lax.rsqrt(jnp.mean(o * o, axis=-1, keepdims=True) + EPS) * hng_ref[...]
            o = o * _silu(z_ref[pl.ds(r0, CHUNK), sl].astype(F32))
            o_ref[pl.ds(r0, CHUNK), sl] = o.astype(o_ref.dtype)
        return carry

    lax.fori_loop(0, rows // CHUNK, chunk_body, 0)


def _delta(proj, ab, conv_w, a_log, dt_bias, head_norm_g, *, batch, seq, n_heads, col0):
    T = proj.shape[0]
    KW = n_heads * HEAD_DIM
    hb = 4 if n_heads % 4 == 0 else 1
    C = hb * HEAD_DIM
    rows = _tile(seq, 256)
    nb = seq // rows
    assert col0 % C == 0 and KW % C == 0 and n_heads <= LANES // 2
    qb, kb_, vb, zb = ((col0 + s * KW) // C for s in range(4))
    row = lambda b, g, c: b * nb + c
    pad = lambda t: jnp.zeros((1, LANES), F32).at[0, :n_heads].set(t)
    nega = pad(-jnp.exp(a_log.astype(F32)))
    dtb = pad(dt_bias.astype(F32))
    wq, wk, wv = (conv_w[:, s * KW:(s + 1) * KW] for s in range(3))
    kern = functools.partial(_delta_kernel, hb=hb, rows=rows, n_heads=n_heads)
    wspec = pl.BlockSpec((CONV_WIDTH, C), lambda b, g, c: (0, g))
    vec = pl.BlockSpec((1, LANES), lambda b, g, c: (0, 0))
    return pl.pallas_call(
        kern,
        grid=(batch, n_heads // hb, nb),
        in_specs=[pl.BlockSpec((rows, C), lambda b, g, c: (row(b, g, c), qb + g)),
                  pl.BlockSpec((rows, C), lambda b, g, c: (row(b, g, c), kb_ + g)),
                  pl.BlockSpec((rows, C), lambda b, g, c: (row(b, g, c), vb + g)),
                  pl.BlockSpec((rows, C), lambda b, g, c: (row(b, g, c), zb + g)),
                  pl.BlockSpec((rows, LANES), lambda b, g, c: (row(b, g, c), 0)),
                  wspec, wspec, wspec, vec, vec, vec],
        out_specs=pl.BlockSpec((rows, C), lambda b, g, c: (row(b, g, c), g)),
        out_shape=jax.ShapeDtypeStruct((T, KW), BF16),
        scratch_shapes=[pltpu.VMEM((hb, HEAD_DIM, HEAD_DIM), F32),
                        pltpu.VMEM((3, 8, C), F32),
                        pltpu.VMEM((rows, C), F32),
                        pltpu.VMEM((rows, C), F32),
                        pltpu.VMEM((rows, C), F32),
                        pltpu.VMEM((rows, LANES), F32),
                        pltpu.VMEM((rows, LANES), F32)],
        compiler_params=_params(("parallel", "parallel", "arbitrary")),
        name="delta",
    )(proj, proj, proj, proj, ab, wq, wk, wv, nega, dtb, head_norm_g.reshape(1, HEAD_DIM).astype(F32))


def _merge_kernel(a1_ref, a2_ref, w1_ref, w2_ref, g1_ref, g2_ref, o_ref):
    y1 = jnp.dot(a1_ref[...], w1_ref[...], preferred_element_type=F32)
    y2 = jnp.dot(a2_ref[...], w2_ref[...], preferred_element_type=F32)
    m = jax.nn.sigmoid(g1_ref[...].astype(F32)) * y1 + jax.nn.sigmoid(g2_ref[...].astype(F32)) * y2
    o_ref[...] = m.astype(o_ref.dtype)


def _merge(yp, yd, w1, w2, proj, gcol0):
    T, K1 = yp.shape
    K2 = yd.shape[1]
    D = w1.shape[1]
    tm, tn = _tile(T, 1024), _tile(D, 512)
    assert gcol0 % tn == 0
    g1b, g2b = gcol0 // tn, (gcol0 + D) // tn
    return pl.pallas_call(
        _merge_kernel,
        grid=(T // tm, D // tn),
        in_specs=[pl.BlockSpec((tm, K1), lambda i, j: (i, 0)),
                  pl.BlockSpec((tm, K2), lambda i, j: (i, 0)),
                  pl.BlockSpec((K1, tn), lambda i, j: (0, j)),
                  pl.BlockSpec((K2, tn), lambda i, j: (0, j)),
                  pl.BlockSpec((tm, tn), lambda i, j: (i, g1b + j)),
                  pl.BlockSpec((tm, tn), lambda i, j: (i, g2b + j))],
        out_specs=pl.BlockSpec((tm, tn), lambda i, j: (i, j)),
        out_shape=jax.ShapeDtypeStruct((T, D), BF16),
        compiler_params=_params(("parallel", "parallel")),
        name="merge",
    )(yp, yd, w1, w2, proj, proj)


def _outproj_kernel(a_ref, w_ref, x_ref, gate_ref, o_ref):
    acc = jnp.dot(a_ref[...], w_ref[...], preferred_element_type=F32)
    o_ref[...] = x_ref[...] + gate_ref[0] * acc


def _outproj(a, w, x2, gate, seq):
    T, K = a.shape
    D = w.shape[1]
    tm, tn = _tile(seq, 1024), _tile(D, 512)
    return pl.pallas_call(
        _outproj_kernel,
        grid=(T // tm, D // tn),
        in_specs=[pl.BlockSpec((tm, K), lambda i, j: (i, 0)),
                  pl.BlockSpec((K, tn), lambda i, j: (0, j)),
                  pl.BlockSpec((tm, tn), lambda i, j: (i, j)),
                  pl.BlockSpec((1, 1, tn), lambda i, j: ((i * tm) // seq, 0, j))],
        out_specs=pl.BlockSpec((tm, tn), lambda i, j: (i, j)),
        out_shape=jax.ShapeDtypeStruct((T, D), F32),
        compiler_params=_params(("parallel", "parallel")),
        name="outproj",
    )(a, w, x2, gate)


def _ffn_kernel(u_ref, w1_ref, w2_ref, h_ref, gate_ref, fg_ref, o_ref, *, nf, tn):
    f = pl.program_id(1)
    @pl.when(f == 0)
    def _():
        o_ref[...] = jnp.zeros_like(o_ref)

    hdn = jnp.maximum(jnp.dot(u_ref[...], w1_ref[...], preferred_element_type=F32), 0.0)
    hb = (hdn * hdn).astype(BF16)
    for n0 in range(0, o_ref.shape[1], tn):
        o_ref[:, n0:n0 + tn] += jnp.dot(hb, w2_ref[:, n0:n0 + tn], preferred_element_type=F32)

    @pl.when(f == nf - 1)
    def _():
        rows = _tile(o_ref.shape[0], 128)

        def body(r, carry):
            sl = pl.ds(pl.multiple_of(r * rows, rows), rows)
            h2 = h_ref[sl, :] + gate_ref[0] * o_ref[sl, :]
            o_ref[sl, :] = h2 * lax.rsqrt(jnp.mean(h2 * h2, axis=-1, keepdims=True) + EPS) * fg_ref[...]
            return carry

        lax.fori_loop(0, o_ref.shape[0] // rows, body, 0)


def _ffn(u2, w1, w2, h, gate, final_g, seq):
    T, D = u2.shape
    F = w1.shape[1]
    tm, tf = _tile(seq, 512), _tile(F, 512)
    nf = F // tf
    return pl.pallas_call(
        functools.partial(_ffn_kernel, nf=nf, tn=_tile(D, 512)),
        grid=(T // tm, nf),
        in_specs=[pl.BlockSpec((tm, D), lambda i, f: (i, 0)),
                  pl.BlockSpec((D, tf), lambda i, f: (0, f)),
                  pl.BlockSpec((tf, D), lambda i, f: (f, 0)),
                  pl.BlockSpec((tm, D), lambda i, f: (i, 0), pipeline_mode=pl.Buffered(1)),
                  pl.BlockSpec((1, 1, D), lambda i, f: ((i * tm) // seq, 0, 0)),
                  pl.BlockSpec((1, D), lambda i, f: (0, 0))],
        out_specs=pl.BlockSpec((tm, D), lambda i, f: (i, 0)),
        out_shape=jax.ShapeDtypeStruct((T, D), F32),
        compiler_params=_params(("parallel", "arbitrary")),
        name="ffn",
    )(u2, w1, w2, h, gate, final_g.reshape(1, D))


def kernel(x, c, w_ada, b_ada, norm1_g, w_in, conv_w, pool_w, pool_scale, a_log, dt_bias,
           head_norm_g, w_up_pool, w_up_delta, w_out, norm2_g, w_ff_in, w_ff_out, final_g):
    B, S, D = x.shape
    T = B * S
    H = a_log.shape[0]
    KW = H * HEAD_DIM
    PW = pool_scale.shape[0]
    assert w_in.shape[1] == PW + 3 * KW + KW + 2 * H + 2 * D
    assert w_ada.shape[1] == N_MOD * D and S % CHUNK == 0 and 2 * H <= LANES

    mod = _ada(c, w_ada, b_ada)
    shift1, scale1, gate1, shift2, scale2, gate2 = (
        mod[:, i * D:(i + 1) * D].reshape(B, 1, D) for i in range(N_MOD))

    c_ab = PW + 4 * KW
    w_main = jnp.concatenate([w_in[:, :c_ab], w_in[:, c_ab + 2 * H:]], axis=1).astype(BF16)
    w_ab = jnp.zeros((D, LANES), F32).at[:, :2 * H].set(w_in[:, c_ab:c_ab + 2 * H]).astype(BF16)

    x2 = x.reshape(T, D)
    u = _norm_mod(x2, norm1_g, shift1, scale1, S)
    proj = _matmul(u, w_main, BF16, name="inproj")
    ab = _matmul(u, w_ab, F32, tn_pref=LANES, name="inproj_ab")

    yp = _pool(proj, pool_w.astype(BF16), pool_scale, S)
    yd = _delta(proj, ab, conv_w, a_log, dt_bias, head_norm_g,
                batch=B, seq=S, n_heads=H, col0=PW)
    merged = _merge(yp, yd, w_up_pool.astype(BF16), w_up_delta.astype(BF16), proj, PW + 4 * KW)
    h = _outproj(merged, w_out.astype(BF16), x2, gate1, S)

    u2 = _norm_mod(h, norm2_g, shift2, scale2, S)
    out = _ffn(u2, w_ff_in.astype(BF16), w_ff_out.astype(BF16), h, gate2, final_g, S)
    return out.reshape(B, S, D)
```

```python
import functools

import jax
import jax.numpy as jnp
from jax import lax
from jax.experimental import pallas as pl
from jax.experimental.pallas import tpu as pltpu

EPS = 1e-6
HEAD_DIM = 128
CHUNK = 64
CONV_WIDTH = 4
POOL_WINDOWS = (2, 4, 8, 16)
N_MOD = 6
LANES = 128
VMEM_LIMIT = 56 * 1024 * 1024
F32 = jnp.float32
BF16 = jnp.bfloat16
HIGHEST = lax.Precision.HIGHEST


def _params(sem):
    return pltpu.CompilerParams(dimension_semantics=sem, vmem_limit_bytes=VMEM_LIMIT)


def _tile(n, pref):
    t = min(n, pref)
    while n % t:
        t //= 2
    return t


def _silu(x):
    return x * jax.nn.sigmoid(x)


def _ada_kernel(c_ref, w_ref, b_ref, o_ref):
    s = _silu(c_ref[...]).astype(BF16)
    o_ref[...] = jnp.dot(s, w_ref[...].astype(BF16), preferred_element_type=F32) + b_ref[...]


def _ada(c, w_ada, b_ada):
    B, D = c.shape
    N = w_ada.shape[1]
    rows = 8
    cp = jnp.zeros((rows, D), F32).at[:B].set(c)
    tn = _tile(N, 512)
    out = pl.pallas_call(
        _ada_kernel,
        grid=(N // tn,),
        in_specs=[pl.BlockSpec((rows, D), lambda j: (0, 0)),
                  pl.BlockSpec((D, tn), lambda j: (0, j)),
                  pl.BlockSpec((1, tn), lambda j: (0, j))],
        out_specs=pl.BlockSpec((rows, tn), lambda j: (0, j)),
        out_shape=jax.ShapeDtypeStruct((rows, N), F32),
        compiler_params=_params(("parallel",)),
        name="ada",
    )(cp, w_ada, b_ada.reshape(1, N))
    return out[:B]


def _norm_kernel(x_ref, g_ref, shift_ref, scale_ref, o_ref):
    x = x_ref[...]
    n = x * lax.rsqrt(jnp.mean(x * x, axis=-1, keepdims=True) + EPS) * g_ref[...]
    o_ref[...] = (n * (1.0 + scale_ref[0]) + shift_ref[0]).astype(o_ref.dtype)


def _norm_mod(x2, g, shift, scale, seq):
    T, D = x2.shape
    tm = _tile(seq, 256)
    bmap = lambda i: ((i * tm) // seq, 0, 0)
    return pl.pallas_call(
        _norm_kernel,
        grid=(T // tm,),
        in_specs=[pl.BlockSpec((tm, D), lambda i: (i, 0)),
                  pl.BlockSpec((1, D), lambda i: (0, 0)),
                  pl.BlockSpec((1, 1, D), bmap),
                  pl.BlockSpec((1, 1, D), bmap)],
        out_specs=pl.BlockSpec((tm, D), lambda i: (i, 0)),
        out_shape=jax.ShapeDtypeStruct((T, D), BF16),
        compiler_params=_params(("parallel",)),
        name="norm_mod",
    )(x2, g.reshape(1, D), shift, scale)


def _col_tiles(w, tn_pref):
    K, N = w.shape
    tn = _tile(N, tn_pref)
    return w.astype(BF16).reshape(K, N // tn, tn).transpose(1, 0, 2)


def _mm_kernel(a_ref, w_ref, o_ref):
    o_ref[...] = jnp.dot(a_ref[...], w_ref[0], preferred_element_type=F32).astype(o_ref.dtype)


def _matmul(a, wt, out_dtype, tm_pref=1024, name="mm"):
    M, K = a.shape
    nj, _, tn = wt.shape
    tm = _tile(M, tm_pref)
    return pl.pallas_call(
        _mm_kernel,
        grid=(M // tm, nj),
        in_specs=[pl.BlockSpec((tm, K), lambda i, j: (i, 0)),
                  pl.BlockSpec((1, K, tn), lambda i, j: (j, 0, 0))],
        out_specs=pl.BlockSpec((tm, tn), lambda i, j: (i, j)),
        out_shape=jax.ShapeDtypeStruct((M, nj * tn), out_dtype),
        compiler_params=_params(("parallel", "parallel")),
        name=name,
    )(a, wt)


def _pool_kernel(p_ref, halo_ref, w_ref, scale_ref, o_ref, *, seq, tm, halo):
    i = pl.program_id(0)
    g = pl.program_id(1)
    start = (i * tm) % seq
    p = p_ref[...]
    hl = jnp.where(start > 0, halo_ref[...], jnp.zeros_like(halo_ref))
    ext = jnp.concatenate([jnp.zeros((LANES - halo, p.shape[1]), p.dtype), hl, p], axis=0)
    win = jnp.where(g == 0, POOL_WINDOWS[0],
                    jnp.where(g == 1, POOL_WINDOWS[1],
                              jnp.where(g == 2, POOL_WINDOWS[2], POOL_WINDOWS[3])))
    r = lax.broadcasted_iota(jnp.int32, (tm, tm + LANES), 0)
    cidx = lax.broadcasted_iota(jnp.int32, (tm, tm + LANES), 1)
    d = r + LANES - cidx
    band = jnp.where((d >= 0) & (d < win), 1.0, 0.0).astype(BF16)
    wsum = jnp.dot(band, ext, preferred_element_type=F32)
    pos = start + lax.broadcasted_iota(jnp.int32, (tm, 1), 0) + 1
    count = jnp.minimum(pos, win).astype(F32)
    mixed = wsum / count - p.astype(F32)
    y = jnp.dot(mixed.astype(BF16), w_ref[0], preferred_element_type=F32)
    o_ref[...] = (y * scale_ref[...]).astype(o_ref.dtype)


def _pool(proj, pool_w_bf, pool_scale, seq):
    T = proj.shape[0]
    G, Cg, _ = pool_w_bf.shape
    assert G == len(POOL_WINDOWS)
    halo = 16
    tm = _tile(seq, 512)
    hb = tm // halo
    kern = functools.partial(_pool_kernel, seq=seq, tm=tm, halo=halo)
    return pl.pallas_call(
        kern,
        grid=(T // tm, G),
        in_specs=[pl.BlockSpec((tm, Cg), lambda i, g: (i, g)),
                  pl.BlockSpec((halo, Cg), lambda i, g: (jnp.maximum(i * hb - 1, 0), g)),
                  pl.BlockSpec((1, Cg, Cg), lambda i, g: (g, 0, 0)),
                  pl.BlockSpec((1, Cg), lambda i, g: (0, g))],
        out_specs=pl.BlockSpec((tm, Cg), lambda i, g: (i, g)),
        out_shape=jax.ShapeDtypeStruct((T, G * Cg), BF16),
        compiler_params=_params(("parallel", "parallel")),
        name="pool",
    )(proj, proj, pool_w_bf, pool_scale.reshape(1, G * Cg))


SOLVE_BASE_LOG2 = 3


def _bdot(a, b):
    return jnp.dot(a.astype(BF16), b.astype(BF16), preferred_element_type=F32)


def _solve_unit_lower(L, X, ri, ci):
    b = SOLVE_BASE_LOG2
    A = jnp.where((ri >> b) == (ci >> b), L, 0.0)
    A2 = _bdot(A, A)
    yield
    A3 = _bdot(A, A2)
    A4 = _bdot(A2, A2)
    yield
    T0 = A2 - A - A3
    Tm = T0 + A4 + _bdot(T0, A4)
    yield
    lvl = b
    while (1 << lvl) < CHUNK:
        Cm = jnp.where(((ri >> (lvl + 1)) == (ci >> (lvl + 1))) & ((ri >> lvl) != (ci >> lvl)), L, 0.0)
        M = Cm + _bdot(Cm, Tm)
        yield
        Tm = Tm - M - _bdot(Tm, M)
        yield
        lvl += 1
    return X + _bdot(Tm, X)


def _round_robin(gens):
    out = [None] * len(gens)
    live = list(range(len(gens)))
    while live:
        for j in list(live):
            try:
                next(gens[j])
            except StopIteration as e:
                out[j] = e.value
                live.remove(j)
    return out


def _delta_kernel(q_ref, k_ref, v_ref, z_ref, ab_ref, wq_ref, wk_ref, wv_ref, nega_ref, dtb_ref,
                  hng_ref, o_ref, s_scr, halo_scr, qs_scr, ks_scr, vs_scr, g_scr, gt_scr, beta_scr,
                  *, hb, rows, n_heads):
    hg = pl.program_id(1)
    cb = pl.program_id(2)
    C = hb * HEAD_DIM

    @pl.when(cb == 0)
    def _():
        s_scr[...] = jnp.zeros_like(s_scr)
        halo_scr[...] = jnp.zeros_like(halo_scr)

    def conv(idx, x_ref, w_ref):
        x = x_ref[...].astype(F32)
        ext = jnp.concatenate([halo_scr[idx], x], axis=0)
        w = w_ref[...]
        y = x * w[CONV_WIDTH - 1:CONV_WIDTH]
        for s in range(1, CONV_WIDTH):
            y = y + ext[8 - s:8 - s + rows] * w[CONV_WIDTH - 1 - s:CONV_WIDTH - s]
        halo_scr[idx] = x[rows - 8:rows]
        return _silu(y)

    def l2n(t, mult):
        outs = []
        for j in range(hb):
            th = t[:, j * HEAD_DIM:(j + 1) * HEAD_DIM]
            outs.append(th * (lax.rsqrt(jnp.sum(th * th, axis=-1, keepdims=True) + EPS) * mult))
        return jnp.concatenate(outs, axis=1) if hb > 1 else outs[0]

    qs_scr[...] = l2n(conv(0, q_ref, wq_ref), HEAD_DIM ** -0.5)
    ks_scr[...] = l2n(conv(1, k_ref, wk_ref), 1.0)
    vs_scr[...] = conv(2, v_ref, wv_ref)

    ri = lax.broadcasted_iota(jnp.int32, (CHUNK, CHUNK), 0)
    ci = lax.broadcasted_iota(jnp.int32, (CHUNK, CHUNK), 1)
    causal = ri >= ci
    strict = ri > ci
    tri = causal.astype(F32)

    ab = ab_ref[...]
    ld = nega_ref[...] * jax.nn.softplus(ab + dtb_ref[...])
    beta_scr[...] = jax.nn.sigmoid(ab)
    for cc in range(rows // CHUNK):
        g = jnp.dot(tri, ld[cc * CHUNK:(cc + 1) * CHUNK], preferred_element_type=F32, precision=HIGHEST)
        g_scr[cc * CHUNK:(cc + 1) * CHUNK, :] = g
        gt_scr[cc] = g.T

    lane = lax.broadcasted_iota(jnp.int32, (CHUNK, LANES), 1)

    hng = hng_ref[...]

    def chunk_head(h, q, k, v, z, S, gch, g_row, btc):
        gb = jnp.broadcast_to(jnp.sum(jnp.where(lane == h, gch, 0.0), axis=-1, keepdims=True),
                              (CHUNK, LANES))
        beta_b = jnp.sum(jnp.where(lane == n_heads + h, btc, 0.0), axis=-1, keepdims=True)
        diff = gb[:, :CHUNK] - g_row
        decay = jnp.where(causal, jnp.exp(jnp.where(causal, diff, 0.0)), 0.0)
        eg = jnp.exp(gb)
        g_last = gb[CHUNK - 1:CHUNK, :]
        ew = jnp.exp(g_last - gb)
        e_last = jnp.exp(g_last)

        kb = k * beta_b
        kq = lax.dot_general(jnp.concatenate([kb, q], axis=0).astype(BF16), k.astype(BF16),
                             (((1,), (1,)), ((), ())), preferred_element_type=F32)
        yield
        L = jnp.where(strict, kq[:CHUNK] * decay, 0.0)
        intra = kq[CHUNK:] * decay
        X = jnp.concatenate([v * beta_b, kb * eg], axis=1)
        X = yield from _solve_unit_lower(L, X, ri, ci)
        yield
        Xb = X.astype(BF16)
        iv = jnp.dot(intra.astype(BF16), Xb, preferred_element_type=F32)
        kv = lax.dot_general((k * ew).astype(BF16), Xb, (((0,), (0,)), ((), ())),
                             preferred_element_type=F32)
        yield
        qe = q * eg - iv[:, HEAD_DIM:]
        lhs = jnp.concatenate([kv[:, HEAD_DIM:], qe], axis=0).astype(BF16)
        rs = jnp.dot(lhs, S.astype(BF16), preferred_element_type=F32)
        yield
        s_new = e_last * S - rs[:HEAD_DIM] + kv[:, :HEAD_DIM]
        o = rs[HEAD_DIM:] + iv[:, :HEAD_DIM]
        o = o * lax.rsqrt(jnp.mean(o * o, axis=-1, keepdims=True) + EPS) * hng
        return s_new, (o * _silu(z)).astype(o_ref.dtype)

    def chunk_body(c, carry):
        rsl = pl.ds(pl.multiple_of(c * CHUNK, CHUNK), CHUNK)
        gch = g_scr[rsl, :]
        btc = beta_scr[rsl, :]
        ins = []
        for j in range(hb):
            sl = slice(j * HEAD_DIM, (j + 1) * HEAD_DIM)
            ins.append((qs_scr[rsl, sl], ks_scr[rsl, sl], vs_scr[rsl, sl],
                        z_ref[rsl, sl].astype(F32), s_scr[j]))
        outs = _round_robin([chunk_head(hg * hb + j, *ins[j], gch, gt_scr[c, pl.ds(hg * hb + j, 1), :], btc)
                             for j in range(hb)])
        for j in range(hb):
            s_scr[j] = outs[j][0]
            o_ref[rsl, j * HEAD_DIM:(j + 1) * HEAD_DIM] = outs[j][1]
        return carry

    lax.fori_loop(0, rows // CHUNK, chunk_body, 0)


def _delta(proj, ab, conv_w, a_log, dt_bias, head_norm_g, *, batch, seq, n_heads, col0):
    T = proj.shape[0]
    KW = n_heads * HEAD_DIM
    hb = next(d for d in (16, 8, 4, 2, 1) if n_heads % d == 0)
    C = hb * HEAD_DIM
    rows = _tile(seq, 256)
    nb = seq // rows
    assert col0 % C == 0 and KW % C == 0 and n_heads <= LANES // 2
    qb, kb_, vb, zb = ((col0 + s * KW) // C for s in range(4))
    row = lambda b, g, c: b * nb + c
    pad = lambda t: jnp.zeros((1, LANES), F32).at[0, :n_heads].set(t)
    nega = pad(-jnp.exp(a_log.astype(F32)))
    dtb = pad(dt_bias.astype(F32))
    wq, wk, wv = (conv_w[:, s * KW:(s + 1) * KW] for s in range(3))
    kern = functools.partial(_delta_kernel, hb=hb, rows=rows, n_heads=n_heads)
    wspec = pl.BlockSpec((CONV_WIDTH, C), lambda b, g, c: (0, g))
    vec = pl.BlockSpec((1, LANES), lambda b, g, c: (0, 0))
    return pl.pallas_call(
        kern,
        grid=(batch, n_heads // hb, nb),
        in_specs=[pl.BlockSpec((rows, C), lambda b, g, c: (row(b, g, c), qb + g)),
                  pl.BlockSpec((rows, C), lambda b, g, c: (row(b, g, c), kb_ + g)),
                  pl.BlockSpec((rows, C), lambda b, g, c: (row(b, g, c), vb + g)),
                  pl.BlockSpec((rows, C), lambda b, g, c: (row(b, g, c), zb + g)),
                  pl.BlockSpec((rows, LANES), lambda b, g, c: (row(b, g, c), 0)),
                  wspec, wspec, wspec, vec, vec, vec],
        out_specs=pl.BlockSpec((rows, C), lambda b, g, c: (row(b, g, c), g)),
        out_shape=jax.ShapeDtypeStruct((T, KW), BF16),
        scratch_shapes=[pltpu.VMEM((hb, HEAD_DIM, HEAD_DIM), F32),
                        pltpu.VMEM((3, 8, C), F32),
                        pltpu.VMEM((rows, C), F32),
                        pltpu.VMEM((rows, C), F32),
                        pltpu.VMEM((rows, C), F32),
                        pltpu.VMEM((rows, LANES), F32),
                        pltpu.VMEM((rows // CHUNK, LANES, CHUNK), F32),
                        pltpu.VMEM((rows, LANES), F32)],
        compiler_params=_params(("parallel", "parallel", "arbitrary")),
        name="delta",
    )(proj, proj, proj, proj, ab, wq, wk, wv, nega, dtb, head_norm_g.reshape(1, HEAD_DIM).astype(F32))


def _merge_kernel(a1_ref, a2_ref, w1_ref, w2_ref, g1_ref, g2_ref, o_ref):
    y1 = jnp.dot(a1_ref[...], w1_ref[0], preferred_element_type=F32)
    y2 = jnp.dot(a2_ref[...], w2_ref[0], preferred_element_type=F32)
    m = jax.nn.sigmoid(g1_ref[...].astype(F32)) * y1 + jax.nn.sigmoid(g2_ref[...].astype(F32)) * y2
    o_ref[...] = m.astype(o_ref.dtype)


def _merge(yp, yd, w1t, w2t, gates):
    T, K1 = yp.shape
    K2 = yd.shape[1]
    nj, _, tn = w1t.shape
    tm = _tile(T, 1024)
    return pl.pallas_call(
        _merge_kernel,
        grid=(T // tm, nj),
        in_specs=[pl.BlockSpec((tm, K1), lambda i, j: (i, 0)),
                  pl.BlockSpec((tm, K2), lambda i, j: (i, 0)),
                  pl.BlockSpec((1, K1, tn), lambda i, j: (j, 0, 0)),
                  pl.BlockSpec((1, K2, tn), lambda i, j: (j, 0, 0)),
                  pl.BlockSpec((tm, tn), lambda i, j: (i, j)),
                  pl.BlockSpec((tm, tn), lambda i, j: (i, nj + j))],
        out_specs=pl.BlockSpec((tm, tn), lambda i, j: (i, j)),
        out_shape=jax.ShapeDtypeStruct((T, nj * tn), BF16),
        compiler_params=_params(("parallel", "parallel")),
        name="merge",
    )(yp, yd, w1t, w2t, gates, gates)


def _outproj_kernel(a_ref, w_ref, x_ref, gate_ref, o_ref):
    acc = jnp.dot(a_ref[...], w_ref[0], preferred_element_type=F32)
    o_ref[...] = x_ref[...] + gate_ref[0] * acc


def _outproj(a, wt, x2, gate, seq):
    T, K = a.shape
    nj, _, tn = wt.shape
    tm = _tile(seq, 1024)
    return pl.pallas_call(
        _outproj_kernel,
        grid=(T // tm, nj),
        in_specs=[pl.BlockSpec((tm, K), lambda i, j: (i, 0)),
                  pl.BlockSpec((1, K, tn), lambda i, j: (j, 0, 0)),
                  pl.BlockSpec((tm, tn), lambda i, j: (i, j)),
                  pl.BlockSpec((1, 1, tn), lambda i, j: ((i * tm) // seq, 0, j))],
        out_specs=pl.BlockSpec((tm, tn), lambda i, j: (i, j)),
        out_shape=jax.ShapeDtypeStruct((T, nj * tn), F32),
        compiler_params=_params(("parallel", "parallel")),
        name="outproj",
    )(a, wt, x2, gate)


def _ffn_kernel(u_ref, w1_ref, w2_ref, h_ref, gate_ref, fg_ref, o_ref, *, nf, tn):
    f = pl.program_id(1)
    @pl.when(f == 0)
    def _():
        o_ref[...] = jnp.zeros_like(o_ref)

    hdn = jnp.maximum(jnp.dot(u_ref[...], w1_ref[0], preferred_element_type=F32), 0.0)
    hb = (hdn * hdn).astype(BF16)
    for n0 in range(0, o_ref.shape[1], tn):
        o_ref[:, n0:n0 + tn] += jnp.dot(hb, w2_ref[:, n0:n0 + tn], preferred_element_type=F32)

    @pl.when(f == nf - 1)
    def _():
        rows = _tile(o_ref.shape[0], 128)

        def body(r, carry):
            sl = pl.ds(pl.multiple_of(r * rows, rows), rows)
            h2 = h_ref[sl, :] + gate_ref[0] * o_ref[sl, :]
            o_ref[sl, :] = h2 * lax.rsqrt(jnp.mean(h2 * h2, axis=-1, keepdims=True) + EPS) * fg_ref[...]
            return carry

        lax.fori_loop(0, o_ref.shape[0] // rows, body, 0)


def _ffn(u2, w1t, w2, h, gate, final_g, seq):
    T, D = u2.shape
    nf, _, tf = w1t.shape
    tm = _tile(seq, 512)
    return pl.pallas_call(
        functools.partial(_ffn_kernel, nf=nf, tn=_tile(D, 512)),
        grid=(T // tm, nf),
        in_specs=[pl.BlockSpec((tm, D), lambda i, f: (i, 0)),
                  pl.BlockSpec((1, D, tf), lambda i, f: (f, 0, 0)),
                  pl.BlockSpec((tf, D), lambda i, f: (f, 0)),
                  pl.BlockSpec((tm, D), lambda i, f: (i, 0), pipeline_mode=pl.Buffered(1)),
                  pl.BlockSpec((1, 1, D), lambda i, f: ((i * tm) // seq, 0, 0)),
                  pl.BlockSpec((1, D), lambda i, f: (0, 0))],
        out_specs=pl.BlockSpec((tm, D), lambda i, f: (i, 0)),
        out_shape=jax.ShapeDtypeStruct((T, D), F32),
        compiler_params=_params(("parallel", "arbitrary")),
        name="ffn",
    )(u2, w1t, w2, h, gate, final_g.reshape(1, D))


def kernel(x, c, w_ada, b_ada, norm1_g, w_in, conv_w, pool_w, pool_scale, a_log, dt_bias,
           head_norm_g, w_up_pool, w_up_delta, w_out, norm2_g, w_ff_in, w_ff_out, final_g):
    B, S, D = x.shape
    T = B * S
    H = a_log.shape[0]
    KW = H * HEAD_DIM
    PW = pool_scale.shape[0]
    assert w_in.shape[1] == PW + 3 * KW + KW + 2 * H + 2 * D
    assert w_ada.shape[1] == N_MOD * D and S % CHUNK == 0 and 2 * H <= LANES

    mod = _ada(c, w_ada, b_ada)
    shift1, scale1, gate1, shift2, scale2, gate2 = (
        mod[:, i * D:(i + 1) * D].reshape(B, 1, D) for i in range(N_MOD))

    c_ab = PW + 4 * KW
    w_proj = _col_tiles(w_in[:, :c_ab], 1024)
    w_gates = _col_tiles(w_in[:, c_ab + 2 * H:], 1024)
    w_ab = _col_tiles(jnp.pad(w_in[:, c_ab:c_ab + 2 * H], ((0, 0), (0, LANES - 2 * H))), LANES)

    x2 = x.reshape(T, D)
    u = _norm_mod(x2, norm1_g, shift1, scale1, S)
    proj = _matmul(u, w_proj, BF16, name="inproj")
    gates = _matmul(u, w_gates, BF16, name="inproj_gates")
    ab = _matmul(u, w_ab, F32, name="inproj_ab")

    yp = _pool(proj, pool_w.astype(BF16), pool_scale, S)
    yd = _delta(proj, ab, conv_w, a_log, dt_bias, head_norm_g,
                batch=B, seq=S, n_heads=H, col0=PW)
    merged = _merge(yp, yd, _col_tiles(w_up_pool, 512), _col_tiles(w_up_delta, 512), gates)
    h = _outproj(merged, _col_tiles(w_out, 512), x2, gate1, S)

    u2 = _norm_mod(h, norm2_g, shift2, scale2, S)
    out = _ffn(u2, _col_tiles(w_ff_in, 512), w_ff_out.astype(BF16), h, gate2, final_g, S)
    return out.reshape(B, S, D)
```

```python
import functools

import jax
import jax.numpy as jnp
from jax import lax
from jax.experimental import pallas as pl
from jax.experimental.pallas import tpu as pltpu

EPS = 1e-6
HEAD_DIM = 128
CHUNK = 64
CONV_WIDTH = 4
POOL_WINDOWS = (2, 4, 8, 16)
N_MOD = 6
LANES = 128
VMEM_LIMIT = 56 * 1024 * 1024
FFN_TM, FFN_TF = 1024, 512
FFN_EPILOGUE_ROWS = 128
F32 = jnp.float32
BF16 = jnp.bfloat16
HIGHEST = lax.Precision.HIGHEST


def _params(sem):
    return pltpu.CompilerParams(dimension_semantics=sem, vmem_limit_bytes=VMEM_LIMIT)


def _tile(n, pref):
    t = min(n, pref)
    while n % t:
        t //= 2
    return t


def _silu(x):
    return x * jax.nn.sigmoid(x)


def _ada_kernel(c_ref, w_ref, b_ref, o_ref):
    s = _silu(c_ref[...]).astype(BF16)
    o_ref[...] = jnp.dot(s, w_ref[...].astype(BF16), preferred_element_type=F32) + b_ref[...]


def _ada(c, w_ada, b_ada):
    B, D = c.shape
    N = w_ada.shape[1]
    rows = 8
    cp = jnp.zeros((rows, D), F32).at[:B].set(c)
    tn = _tile(N, 512)
    out = pl.pallas_call(
        _ada_kernel,
        grid=(N // tn,),
        in_specs=[pl.BlockSpec((rows, D), lambda j: (0, 0)),
                  pl.BlockSpec((D, tn), lambda j: (0, j)),
                  pl.BlockSpec((1, tn), lambda j: (0, j))],
        out_specs=pl.BlockSpec((rows, tn), lambda j: (0, j)),
        out_shape=jax.ShapeDtypeStruct((rows, N), F32),
        compiler_params=_params(("parallel",)),
        name="ada",
    )(cp, w_ada, b_ada.reshape(1, N))
    return out[:B]


def _norm_kernel(x_ref, g_ref, shift_ref, scale_ref, o_ref):
    x = x_ref[...]
    n = x * lax.rsqrt(jnp.mean(x * x, axis=-1, keepdims=True) + EPS) * g_ref[...]
    o_ref[...] = (n * (1.0 + scale_ref[0]) + shift_ref[0]).astype(o_ref.dtype)


def _norm_mod(x2, g, shift, scale, seq):
    T, D = x2.shape
    tm = _tile(seq, 256)
    bmap = lambda i: ((i * tm) // seq, 0, 0)
    return pl.pallas_call(
        _norm_kernel,
        grid=(T // tm,),
        in_specs=[pl.BlockSpec((tm, D), lambda i: (i, 0)),
                  pl.BlockSpec((1, D), lambda i: (0, 0)),
                  pl.BlockSpec((1, 1, D), bmap),
                  pl.BlockSpec((1, 1, D), bmap)],
        out_specs=pl.BlockSpec((tm, D), lambda i: (i, 0)),
        out_shape=jax.ShapeDtypeStruct((T, D), BF16),
        compiler_params=_params(("parallel",)),
        name="norm_mod",
    )(x2, g.reshape(1, D), shift, scale)


def _mm_kernel(a_ref, w_ref, o_ref):
    o_ref[...] = jnp.dot(a_ref[...], w_ref[...], preferred_element_type=F32).astype(o_ref.dtype)


def _matmul(a, w, out_dtype, n_cols=None, tm_pref=1024, tn_pref=1024, name="mm"):
    M, K = a.shape
    N = w.shape[1] if n_cols is None else n_cols
    tm, tn = _tile(M, tm_pref), _tile(N, tn_pref)
    return pl.pallas_call(
        _mm_kernel,
        grid=(M // tm, N // tn),
        in_specs=[pl.BlockSpec((tm, K), lambda i, j: (i, 0)),
                  pl.BlockSpec((K, tn), lambda i, j: (0, j))],
        out_specs=pl.BlockSpec((tm, tn), lambda i, j: (i, j)),
        out_shape=jax.ShapeDtypeStruct((M, N), out_dtype),
        compiler_params=_params(("parallel", "parallel")),
        name=name,
    )(a, w)


def _pool_kernel(p_ref, halo_ref, w_ref, scale_ref, o_ref, *, seq, tm, halo):
    i = pl.program_id(0)
    g = pl.program_id(1)
    start = (i * tm) % seq
    p = p_ref[...]
    hl = jnp.where(start > 0, halo_ref[...], jnp.zeros_like(halo_ref))
    ext = jnp.concatenate([jnp.zeros((LANES - halo, p.shape[1]), p.dtype), hl, p], axis=0)
    win = jnp.where(g == 0, POOL_WINDOWS[0],
                    jnp.where(g == 1, POOL_WINDOWS[1],
                              jnp.where(g == 2, POOL_WINDOWS[2], POOL_WINDOWS[3])))
    r = lax.broadcasted_iota(jnp.int32, (tm, tm + LANES), 0)
    cidx = lax.broadcasted_iota(jnp.int32, (tm, tm + LANES), 1)
    d = r + LANES - cidx
    band = jnp.where((d >= 0) & (d < win), 1.0, 0.0).astype(BF16)
    wsum = jnp.dot(band, ext, preferred_element_type=F32)
    pos = start + lax.broadcasted_iota(jnp.int32, (tm, 1), 0) + 1
    count = jnp.minimum(pos, win).astype(F32)
    mixed = wsum / count - p.astype(F32)
    y = jnp.dot(mixed.astype(BF16), w_ref[0], preferred_element_type=F32)
    o_ref[...] = (y * scale_ref[...]).astype(o_ref.dtype)


def _pool(proj, pool_w_bf, pool_scale, seq):
    T = proj.shape[0]
    G, Cg, _ = pool_w_bf.shape
    assert G == len(POOL_WINDOWS)
    halo = 16
    tm = _tile(seq, 512)
    hb = tm // halo
    kern = functools.partial(_pool_kernel, seq=seq, tm=tm, halo=halo)
    return pl.pallas_call(
        kern,
        grid=(T // tm, G),
        in_specs=[pl.BlockSpec((tm, Cg), lambda i, g: (i, g)),
                  pl.BlockSpec((halo, Cg), lambda i, g: (jnp.maximum(i * hb - 1, 0), g)),
                  pl.BlockSpec((1, Cg, Cg), lambda i, g: (g, 0, 0)),
                  pl.BlockSpec((1, Cg), lambda i, g: (0, g))],
        out_specs=pl.BlockSpec((tm, Cg), lambda i, g: (i, g)),
        out_shape=jax.ShapeDtypeStruct((T, G * Cg), BF16),
        compiler_params=_params(("parallel", "parallel")),
        name="pool",
    )(proj, proj, pool_w_bf, pool_scale.reshape(1, G * Cg))


SOLVE_BASE_LOG2 = 3


def _bdot(a, b):
    return jnp.dot(a.astype(BF16), b.astype(BF16), preferred_element_type=F32)


def _solve_unit_lower(L, X, ri, ci):
    b = SOLVE_BASE_LOG2
    A = jnp.where((ri >> b) == (ci >> b), L, 0.0)
    A2 = _bdot(A, A)
    yield
    A3 = _bdot(A, A2)
    A4 = _bdot(A2, A2)
    yield
    T0 = A2 - A - A3
    Tm = T0 + A4 + _bdot(T0, A4)
    yield
    lvl = b
    while (1 << lvl) < CHUNK:
        Cm = jnp.where(((ri >> (lvl + 1)) == (ci >> (lvl + 1))) & ((ri >> lvl) != (ci >> lvl)), L, 0.0)
        M = Cm + _bdot(Cm, Tm)
        yield
        Tm = Tm - M - _bdot(Tm, M)
        yield
        lvl += 1
    return X + _bdot(Tm, X)


def _round_robin(gens):
    out = [None] * len(gens)
    live = list(range(len(gens)))
    while live:
        for j in list(live):
            try:
                next(gens[j])
            except StopIteration as e:
                out[j] = e.value
                live.remove(j)
    return out


def _delta_kernel(q_ref, k_ref, v_ref, z_ref, ab_ref, wq_ref, wk_ref, wv_ref, nega_ref, dtb_ref,
                  hng_ref, o_ref, s_scr, halo_scr, qs_scr, ks_scr, vs_scr, g_scr, gt_scr, beta_scr,
                  *, hb, rows, n_heads):
    hg = pl.program_id(1)
    cb = pl.program_id(2)
    C = hb * HEAD_DIM

    @pl.when(cb == 0)
    def _():
        s_scr[...] = jnp.zeros_like(s_scr)
        halo_scr[...] = jnp.zeros_like(halo_scr)

    def conv(idx, x_ref, w_ref):
        x = x_ref[...].astype(F32)
        ext = jnp.concatenate([halo_scr[idx], x], axis=0)
        w = w_ref[...]
        y = x * w[CONV_WIDTH - 1:CONV_WIDTH]
        for s in range(1, CONV_WIDTH):
            y = y + ext[8 - s:8 - s + rows] * w[CONV_WIDTH - 1 - s:CONV_WIDTH - s]
        halo_scr[idx] = x[rows - 8:rows]
        return _silu(y)

    def l2n(t, mult):
        outs = []
        for j in range(hb):
            th = t[:, j * HEAD_DIM:(j + 1) * HEAD_DIM]
            outs.append(th * (lax.rsqrt(jnp.sum(th * th, axis=-1, keepdims=True) + EPS) * mult))
        return jnp.concatenate(outs, axis=1) if hb > 1 else outs[0]

    qs_scr[...] = l2n(conv(0, q_ref, wq_ref), HEAD_DIM ** -0.5)
    ks_scr[...] = l2n(conv(1, k_ref, wk_ref), 1.0)
    vs_scr[...] = conv(2, v_ref, wv_ref)

    ri = lax.broadcasted_iota(jnp.int32, (CHUNK, CHUNK), 0)
    ci = lax.broadcasted_iota(jnp.int32, (CHUNK, CHUNK), 1)
    causal = ri >= ci
    strict = ri > ci
    tri = causal.astype(F32)

    ab = ab_ref[...]
    ld = nega_ref[...] * jax.nn.softplus(ab + dtb_ref[...])
    beta_scr[...] = jax.nn.sigmoid(ab)
    for cc in range(rows // CHUNK):
        g = jnp.dot(tri, ld[cc * CHUNK:(cc + 1) * CHUNK], preferred_element_type=F32, precision=HIGHEST)
        g_scr[cc * CHUNK:(cc + 1) * CHUNK, :] = g
        gt_scr[cc] = g.T

    lane = lax.broadcasted_iota(jnp.int32, (CHUNK, LANES), 1)

    hng = hng_ref[...]

    def chunk_head(h, q, k, v, z, S, gch, g_row, btc):
        gb = jnp.broadcast_to(jnp.sum(jnp.where(lane == h, gch, 0.0), axis=-1, keepdims=True),
                              (CHUNK, LANES))
        beta_b = jnp.sum(jnp.where(lane == n_heads + h, btc, 0.0), axis=-1, keepdims=True)
        diff = gb[:, :CHUNK] - g_row
        decay = jnp.where(causal, jnp.exp(jnp.where(causal, diff, 0.0)), 0.0)
        eg = jnp.exp(gb)
        g_last = gb[CHUNK - 1:CHUNK, :]
        ew = jnp.exp(g_last - gb)
        e_last = jnp.exp(g_last)

        kb = k * beta_b
        kq = lax.dot_general(jnp.concatenate([kb, q], axis=0).astype(BF16), k.astype(BF16),
                             (((1,), (1,)), ((), ())), preferred_element_type=F32)
        yield
        L = jnp.where(strict, kq[:CHUNK] * decay, 0.0)
        intra = kq[CHUNK:] * decay
        X = jnp.concatenate([v * beta_b, kb * eg], axis=1)
        X = yield from _solve_unit_lower(L, X, ri, ci)
        yield
        Xb = X.astype(BF16)
        iv = jnp.dot(intra.astype(BF16), Xb, preferred_element_type=F32)
        kv = lax.dot_general((k * ew).astype(BF16), Xb, (((0,), (0,)), ((), ())),
                             preferred_element_type=F32)
        yield
        qe = q * eg - iv[:, HEAD_DIM:]
        lhs = jnp.concatenate([kv[:, HEAD_DIM:], qe], axis=0).astype(BF16)
        rs = jnp.dot(lhs, S.astype(BF16), preferred_element_type=F32)
        yield
        s_new = e_last * S - rs[:HEAD_DIM] + kv[:, :HEAD_DIM]
        o = rs[HEAD_DIM:] + iv[:, :HEAD_DIM]
        o = o * lax.rsqrt(jnp.mean(o * o, axis=-1, keepdims=True) + EPS) * hng
        return s_new, (o * _silu(z)).astype(o_ref.dtype)

    def chunk_body(c, carry):
        rsl = pl.ds(pl.multiple_of(c * CHUNK, CHUNK), CHUNK)
        gch = g_scr[rsl, :]
        btc = beta_scr[rsl, :]
        ins = []
        for j in range(hb):
            sl = slice(j * HEAD_DIM, (j + 1) * HEAD_DIM)
            ins.append((qs_scr[rsl, sl], ks_scr[rsl, sl], vs_scr[rsl, sl],
                        z_ref[rsl, sl].astype(F32), s_scr[j]))
        outs = _round_robin([chunk_head(hg * hb + j, *ins[j], gch, gt_scr[c, pl.ds(hg * hb + j, 1), :], btc)
                             for j in range(hb)])
        for j in range(hb):
            s_scr[j] = outs[j][0]
            o_ref[rsl, j * HEAD_DIM:(j + 1) * HEAD_DIM] = outs[j][1]
        return carry

    lax.fori_loop(0, rows // CHUNK, chunk_body, 0)


def _delta(proj, ab, conv_w, a_log, dt_bias, head_norm_g, *, batch, seq, n_heads, col0):
    T = proj.shape[0]
    KW = n_heads * HEAD_DIM
    hb = next(d for d in (16, 8, 4, 2, 1) if n_heads % d == 0)
    C = hb * HEAD_DIM
    rows = _tile(seq, 256)
    nb = seq // rows
    assert col0 % C == 0 and KW % C == 0 and n_heads <= LANES // 2
    qb, kb_, vb, zb = ((col0 + s * KW) // C for s in range(4))
    row = lambda b, g, c: b * nb + c
    pad = lambda t: jnp.zeros((1, LANES), F32).at[0, :n_heads].set(t)
    nega = pad(-jnp.exp(a_log.astype(F32)))
    dtb = pad(dt_bias.astype(F32))
    wq, wk, wv = (conv_w[:, s * KW:(s + 1) * KW] for s in range(3))
    kern = functools.partial(_delta_kernel, hb=hb, rows=rows, n_heads=n_heads)
    wspec = pl.BlockSpec((CONV_WIDTH, C), lambda b, g, c: (0, g))
    vec = pl.BlockSpec((1, LANES), lambda b, g, c: (0, 0))
    return pl.pallas_call(
        kern,
        grid=(batch, n_heads // hb, nb),
        in_specs=[pl.BlockSpec((rows, C), lambda b, g, c: (row(b, g, c), qb + g)),
                  pl.BlockSpec((rows, C), lambda b, g, c: (row(b, g, c), kb_ + g)),
                  pl.BlockSpec((rows, C), lambda b, g, c: (row(b, g, c), vb + g)),
                  pl.BlockSpec((rows, C), lambda b, g, c: (row(b, g, c), zb + g)),
                  pl.BlockSpec((rows, LANES), lambda b, g, c: (row(b, g, c), 0)),
                  wspec, wspec, wspec, vec, vec, vec],
        out_specs=pl.BlockSpec((rows, C), lambda b, g, c: (row(b, g, c), g)),
        out_shape=jax.ShapeDtypeStruct((T, KW), BF16),
        scratch_shapes=[pltpu.VMEM((hb, HEAD_DIM, HEAD_DIM), F32),
                        pltpu.VMEM((3, 8, C), F32),
                        pltpu.VMEM((rows, C), F32),
                        pltpu.VMEM((rows, C), F32),
                        pltpu.VMEM((rows, C), F32),
                        pltpu.VMEM((rows, LANES), F32),
                        pltpu.VMEM((rows // CHUNK, LANES, CHUNK), F32),
                        pltpu.VMEM((rows, LANES), F32)],
        compiler_params=_params(("parallel", "parallel", "arbitrary")),
        name="delta",
    )(proj, proj, proj, proj, ab, wq, wk, wv, nega, dtb, head_norm_g.reshape(1, HEAD_DIM).astype(F32))


def _merge_kernel(a1_ref, a2_ref, w1_ref, w2_ref, g1_ref, g2_ref, o_ref):
    y1 = jnp.dot(a1_ref[...], w1_ref[...], preferred_element_type=F32)
    y2 = jnp.dot(a2_ref[...], w2_ref[...], preferred_element_type=F32)
    m = jax.nn.sigmoid(g1_ref[...].astype(F32)) * y1 + jax.nn.sigmoid(g2_ref[...].astype(F32)) * y2
    o_ref[...] = m.astype(o_ref.dtype)


def _merge(yp, yd, w1, w2, gates):
    T, K1 = yp.shape
    K2 = yd.shape[1]
    D = w1.shape[1]
    tm, tn = _tile(T, 1024), _tile(D, 512)
    nj = D // tn
    return pl.pallas_call(
        _merge_kernel,
        grid=(T // tm, nj),
        in_specs=[pl.BlockSpec((tm, K1), lambda i, j: (i, 0)),
                  pl.BlockSpec((tm, K2), lambda i, j: (i, 0)),
                  pl.BlockSpec((K1, tn), lambda i, j: (0, j)),
                  pl.BlockSpec((K2, tn), lambda i, j: (0, j)),
                  pl.BlockSpec((tm, tn), lambda i, j: (i, j)),
                  pl.BlockSpec((tm, tn), lambda i, j: (i, nj + j))],
        out_specs=pl.BlockSpec((tm, tn), lambda i, j: (i, j)),
        out_shape=jax.ShapeDtypeStruct((T, D), BF16),
        compiler_params=_params(("parallel", "parallel")),
        name="merge",
    )(yp, yd, w1, w2, gates, gates)


def _outproj_kernel(a_ref, w_ref, x_ref, gate_ref, o_ref):
    acc = jnp.dot(a_ref[...], w_ref[...], preferred_element_type=F32)
    o_ref[...] = x_ref[...] + gate_ref[0] * acc


def _outproj(a, w, x2, gate, seq):
    T, K = a.shape
    D = w.shape[1]
    tm, tn = _tile(seq, 1024), _tile(D, 512)
    return pl.pallas_call(
        _outproj_kernel,
        grid=(T // tm, D // tn),
        in_specs=[pl.BlockSpec((tm, K), lambda i, j: (i, 0)),
                  pl.BlockSpec((K, tn), lambda i, j: (0, j)),
                  pl.BlockSpec((tm, tn), lambda i, j: (i, j)),
                  pl.BlockSpec((1, 1, tn), lambda i, j: ((i * tm) // seq, 0, j))],
        out_specs=pl.BlockSpec((tm, tn), lambda i, j: (i, j)),
        out_shape=jax.ShapeDtypeStruct((T, D), F32),
        compiler_params=_params(("parallel", "parallel")),
        name="outproj",
    )(a, w, x2, gate)


def _ffn_kernel(u_ref, w1_ref, w2_ref, h_hbm, gate_ref, fg_ref, o_ref, hbuf, hsem, *, nf, tn):
    i = pl.program_id(0)
    f = pl.program_id(1)
    tm = o_ref.shape[0]
    rows = hbuf.shape[1]
    nchunk = tm // rows

    def h_copy(r, slot):
        return pltpu.make_async_copy(h_hbm.at[pl.ds(i * tm + r * rows, rows), :], hbuf.at[slot], hsem.at[slot])

    @pl.when(f == 0)
    def _():
        o_ref[...] = jnp.zeros_like(o_ref)

    @pl.when(f == nf - 1)
    def _():
        h_copy(0, 0).start()

    hdn = jnp.maximum(jnp.dot(u_ref[...], w1_ref[...], preferred_element_type=F32), 0.0)
    hb = (hdn * hdn).astype(BF16)
    for n0 in range(0, o_ref.shape[1], tn):
        o_ref[:, n0:n0 + tn] += jnp.dot(hb, w2_ref[:, n0:n0 + tn], preferred_element_type=F32)

    @pl.when(f == nf - 1)
    def _():
        def body(r, carry):
            slot = lax.rem(r, 2)

            @pl.when(r + 1 < nchunk)
            def _():
                h_copy(r + 1, 1 - slot).start()

            h_copy(r, slot).wait()
            sl = pl.ds(pl.multiple_of(r * rows, rows), rows)
            h2 = hbuf[slot] + gate_ref[0] * o_ref[sl, :]
            o_ref[sl, :] = h2 * lax.rsqrt(jnp.mean(h2 * h2, axis=-1, keepdims=True) + EPS) * fg_ref[...]
            return carry

        lax.fori_loop(0, nchunk, body, 0)


def _ffn(u2, w1, w2, h, gate, final_g, seq):
    T, D = u2.shape
    F = w1.shape[1]
    tm, tf = _tile(seq, FFN_TM), _tile(F, FFN_TF)
    nf = F // tf
    once = dict(pipeline_mode=pl.Buffered(1))
    return pl.pallas_call(
        functools.partial(_ffn_kernel, nf=nf, tn=_tile(D, 512)),
        grid=(T // tm, nf),
        in_specs=[pl.BlockSpec((tm, D), lambda i, f: (i, 0), **once),
                  pl.BlockSpec((D, tf), lambda i, f: (0, f)),
                  pl.BlockSpec((tf, D), lambda i, f: (f, 0)),
                  pl.BlockSpec(memory_space=pl.ANY),
                  pl.BlockSpec((1, 1, D), lambda i, f: ((i * tm) // seq, 0, 0)),
                  pl.BlockSpec((1, D), lambda i, f: (0, 0))],
        out_specs=pl.BlockSpec((tm, D), lambda i, f: (i, 0), **once),
        out_shape=jax.ShapeDtypeStruct((T, D), F32),
        scratch_shapes=[pltpu.VMEM((2, _tile(tm, FFN_EPILOGUE_ROWS), D), F32),
                        pltpu.SemaphoreType.DMA((2,))],
        compiler_params=_params(("arbitrary", "arbitrary")),
        name="ffn",
    )(u2, w1, w2, h, gate, final_g.reshape(1, D))


def kernel(x, c, w_ada, b_ada, norm1_g, w_in, conv_w, pool_w, pool_scale, a_log, dt_bias,
           head_norm_g, w_up_pool, w_up_delta, w_out, norm2_g, w_ff_in, w_ff_out, final_g):
    B, S, D = x.shape
    T = B * S
    H = a_log.shape[0]
    KW = H * HEAD_DIM
    PW = pool_scale.shape[0]
    assert w_in.shape[1] == PW + 3 * KW + KW + 2 * H + 2 * D
    assert w_ada.shape[1] == N_MOD * D and S % CHUNK == 0 and 2 * H <= LANES

    mod = _ada(c, w_ada, b_ada)
    shift1, scale1, gate1, shift2, scale2, gate2 = (
        mod[:, i * D:(i + 1) * D].reshape(B, 1, D) for i in range(N_MOD))

    c_ab = PW + 4 * KW
    w_bf = w_in.astype(BF16)
    w_gates = w_bf[:, c_ab + 2 * H:]
    w_ab = jnp.pad(w_bf[:, c_ab:c_ab + 2 * H], ((0, 0), (0, LANES - 2 * H)))

    x2 = x.reshape(T, D)
    u = _norm_mod(x2, norm1_g, shift1, scale1, S)
    proj = _matmul(u, w_bf, BF16, n_cols=c_ab, name="inproj")
    gates = _matmul(u, w_gates, BF16, name="inproj_gates")
    ab = _matmul(u, w_ab, F32, name="inproj_ab")

    yp = _pool(proj, pool_w.astype(BF16), pool_scale, S)
    yd = _delta(proj, ab, conv_w, a_log, dt_bias, head_norm_g,
                batch=B, seq=S, n_heads=H, col0=PW)
    merged = _merge(yp, yd, w_up_pool.astype(BF16), w_up_delta.astype(BF16), gates)
    h = _outproj(merged, w_out.astype(BF16), x2, gate1, S)

    u2 = _norm_mod(h, norm2_g, shift2, scale2, S)
    out = _ffn(u2, w_ff_in.astype(BF16), w_ff_out.astype(BF16), h, gate2, final_g, S)
    return out.reshape(B, S, D)
```

```python
import functools

import jax
import jax.numpy as jnp
from jax import lax
from jax.experimental import pallas as pl
from jax.experimental.pallas import tpu as pltpu

EPS = 1e-6
HEAD_DIM = 128
CHUNK = 64
CONV_WIDTH = 4
POOL_WINDOWS = (2, 4, 8, 16)
N_MOD = 6
LANES = 128
VMEM_LIMIT = 56 * 1024 * 1024
FFN_TM, FFN_TF = 1024, 512
FFN_EPILOGUE_ROWS = 128
F32 = jnp.float32
BF16 = jnp.bfloat16
HIGHEST = lax.Precision.HIGHEST


def _params(sem):
    return pltpu.CompilerParams(dimension_semantics=sem, vmem_limit_bytes=VMEM_LIMIT)


def _tile(n, pref):
    t = min(n, pref)
    while n % t:
        t //= 2
    return t


def _silu(x):
    return x * jax.nn.sigmoid(x)


def _ada_kernel(c_ref, w_ref, b_ref, o_ref):
    s = _silu(c_ref[...]).astype(BF16)
    o_ref[...] = jnp.dot(s, w_ref[...].astype(BF16), preferred_element_type=F32) + b_ref[...]


def _ada(c, w_ada, b_ada):
    B, D = c.shape
    N = w_ada.shape[1]
    rows = 8
    cp = jnp.zeros((rows, D), F32).at[:B].set(c)
    tn = _tile(N, 512)
    out = pl.pallas_call(
        _ada_kernel,
        grid=(N // tn,),
        in_specs=[pl.BlockSpec((rows, D), lambda j: (0, 0)),
                  pl.BlockSpec((D, tn), lambda j: (0, j)),
                  pl.BlockSpec((1, tn), lambda j: (0, j))],
        out_specs=pl.BlockSpec((rows, tn), lambda j: (0, j)),
        out_shape=jax.ShapeDtypeStruct((rows, N), F32),
        compiler_params=_params(("parallel",)),
        name="ada",
    )(cp, w_ada, b_ada.reshape(1, N))
    return out[:B]


def _norm_kernel(x_ref, g_ref, shift_ref, scale_ref, o_ref):
    x = x_ref[...]
    n = x * lax.rsqrt(jnp.mean(x * x, axis=-1, keepdims=True) + EPS) * g_ref[...]
    o_ref[...] = (n * (1.0 + scale_ref[0]) + shift_ref[0]).astype(o_ref.dtype)


def _norm_mod(x2, g, shift, scale, seq):
    T, D = x2.shape
    tm = _tile(seq, 512)
    bmap = lambda i: ((i * tm) // seq, 0, 0)
    return pl.pallas_call(
        _norm_kernel,
        grid=(T // tm,),
        in_specs=[pl.BlockSpec((tm, D), lambda i: (i, 0)),
                  pl.BlockSpec((1, D), lambda i: (0, 0)),
                  pl.BlockSpec((1, 1, D), bmap),
                  pl.BlockSpec((1, 1, D), bmap)],
        out_specs=pl.BlockSpec((tm, D), lambda i: (i, 0)),
        out_shape=jax.ShapeDtypeStruct((T, D), BF16),
        compiler_params=_params(("parallel",)),
        name="norm_mod",
    )(x2, g.reshape(1, D), shift, scale)


def _mm_kernel(a_ref, w_ref, *refs):
    n_side = (len(refs) - 1) // 2
    o_ref = refs[n_side]
    o_ref[...] = jnp.dot(a_ref[...], w_ref[...], preferred_element_type=F32).astype(o_ref.dtype)
    for src, dst in zip(refs[:n_side], refs[n_side + 1:]):
        dst[...] = src[...].astype(dst.dtype)


def _matmul(a, w, out_dtype, tm_pref=1024, tn_pref=1024, name="mm", cast_along=()):
    M, K = a.shape
    N = w.shape[1]
    tm, tn = _tile(M, tm_pref), _tile(N, tn_pref)
    ni, nj = M // tm, N // tn
    steps = ni * nj
    side_specs = []
    for t in cast_along:
        rows = t.shape[0] // steps
        assert rows * steps == t.shape[0] and rows % 16 == 0, (t.shape, steps)
        side_specs.append(pl.BlockSpec((rows, t.shape[1]), lambda i, j: (i * nj + j, 0)))
    outs = pl.pallas_call(
        _mm_kernel,
        grid=(ni, nj),
        in_specs=[pl.BlockSpec((tm, K), lambda i, j: (i, 0)),
                  pl.BlockSpec((K, tn), lambda i, j: (0, j))] + side_specs,
        out_specs=[pl.BlockSpec((tm, tn), lambda i, j: (i, j))] + side_specs,
        out_shape=[jax.ShapeDtypeStruct((M, N), out_dtype)]
                  + [jax.ShapeDtypeStruct(t.shape, BF16) for t in cast_along],
        compiler_params=_params(("arbitrary", "arbitrary")),
        name=name,
    )(a, w, *cast_along)
    return (outs[0], outs[1:]) if cast_along else outs[0]


def _pool_kernel(p_ref, halo_ref, w_ref, scale_ref, o_ref, *, seq, tm, halo):
    i = pl.program_id(0)
    g = pl.program_id(1)
    start = (i * tm) % seq
    p = p_ref[...]
    hl = jnp.where(start > 0, halo_ref[...], jnp.zeros_like(halo_ref))
    ext = jnp.concatenate([jnp.zeros((LANES - halo, p.shape[1]), p.dtype), hl, p], axis=0)
    win = jnp.where(g == 0, POOL_WINDOWS[0],
                    jnp.where(g == 1, POOL_WINDOWS[1],
                              jnp.where(g == 2, POOL_WINDOWS[2], POOL_WINDOWS[3])))
    r = lax.broadcasted_iota(jnp.int32, (tm, tm + LANES), 0)
    cidx = lax.broadcasted_iota(jnp.int32, (tm, tm + LANES), 1)
    d = r + LANES - cidx
    band = jnp.where((d >= 0) & (d < win), 1.0, 0.0).astype(BF16)
    wsum = jnp.dot(band, ext, preferred_element_type=F32)
    pos = start + lax.broadcasted_iota(jnp.int32, (tm, 1), 0) + 1
    count = jnp.minimum(pos, win).astype(F32)
    mixed = wsum / count - p.astype(F32)
    y = jnp.dot(mixed.astype(BF16), w_ref[0], preferred_element_type=F32)
    o_ref[...] = (y * scale_ref[...]).astype(o_ref.dtype)


def _pool(proj, pool_w_bf, pool_scale, seq):
    T = proj.shape[0]
    G, Cg, _ = pool_w_bf.shape
    assert G == len(POOL_WINDOWS)
    halo = 16
    tm = _tile(seq, 512)
    hb = tm // halo
    kern = functools.partial(_pool_kernel, seq=seq, tm=tm, halo=halo)
    return pl.pallas_call(
        kern,
        grid=(T // tm, G),
        in_specs=[pl.BlockSpec((tm, Cg), lambda i, g: (i, g)),
                  pl.BlockSpec((halo, Cg), lambda i, g: (jnp.maximum(i * hb - 1, 0), g)),
                  pl.BlockSpec((1, Cg, Cg), lambda i, g: (g, 0, 0)),
                  pl.BlockSpec((1, Cg), lambda i, g: (0, g))],
        out_specs=pl.BlockSpec((tm, Cg), lambda i, g: (i, g)),
        out_shape=jax.ShapeDtypeStruct((T, G * Cg), BF16),
        compiler_params=_params(("parallel", "parallel")),
        name="pool",
    )(proj, proj, pool_w_bf, pool_scale.reshape(1, G * Cg))


SOLVE_BASE_LOG2 = 3


def _bdot(a, b):
    return jnp.dot(a.astype(BF16), b.astype(BF16), preferred_element_type=F32)


def _solve_unit_lower(L, X, ri, ci):
    b = SOLVE_BASE_LOG2
    A = jnp.where((ri >> b) == (ci >> b), L, 0.0)
    A2 = _bdot(A, A)
    yield
    A3 = _bdot(A, A2)
    A4 = _bdot(A2, A2)
    yield
    T0 = A2 - A - A3
    Tm = T0 + A4 + _bdot(T0, A4)
    yield
    lvl = b
    while (1 << lvl) < CHUNK:
        Cm = jnp.where(((ri >> (lvl + 1)) == (ci >> (lvl + 1))) & ((ri >> lvl) != (ci >> lvl)), L, 0.0)
        M = Cm + _bdot(Cm, Tm)
        yield
        Tm = Tm - M - _bdot(Tm, M)
        yield
        lvl += 1
    return X + _bdot(Tm, X)


def _round_robin(gens):
    out = [None] * len(gens)
    live = list(range(len(gens)))
    while live:
        for j in list(live):
            try:
                next(gens[j])
            except StopIteration as e:
                out[j] = e.value
                live.remove(j)
    return out


def _delta_kernel(q_ref, k_ref, v_ref, z_ref, ab_ref, wq_ref, wk_ref, wv_ref, nega_ref, dtb_ref,
                  hng_ref, o_ref, s_scr, halo_scr, qs_scr, ks_scr, vs_scr, g_scr, gt_scr, beta_scr,
                  *, hb, rows, n_heads):
    hg = pl.program_id(1)
    cb = pl.program_id(2)
    C = hb * HEAD_DIM

    @pl.when(cb == 0)
    def _():
        s_scr[...] = jnp.zeros_like(s_scr)
        halo_scr[...] = jnp.zeros_like(halo_scr)

    def conv(idx, x_ref, w_ref):
        x = x_ref[...].astype(F32)
        ext = jnp.concatenate([halo_scr[idx], x], axis=0)
        w = w_ref[...]
        y = x * w[CONV_WIDTH - 1:CONV_WIDTH]
        for s in range(1, CONV_WIDTH):
            y = y + ext[8 - s:8 - s + rows] * w[CONV_WIDTH - 1 - s:CONV_WIDTH - s]
        halo_scr[idx] = x[rows - 8:rows]
        return _silu(y)

    def l2n(t, mult):
        outs = []
        for j in range(hb):
            th = t[:, j * HEAD_DIM:(j + 1) * HEAD_DIM]
            outs.append(th * (lax.rsqrt(jnp.sum(th * th, axis=-1, keepdims=True) + EPS) * mult))
        return jnp.concatenate(outs, axis=1) if hb > 1 else outs[0]

    qs_scr[...] = l2n(conv(0, q_ref, wq_ref), HEAD_DIM ** -0.5)
    ks_scr[...] = l2n(conv(1, k_ref, wk_ref), 1.0)
    vs_scr[...] = conv(2, v_ref, wv_ref)

    ri = lax.broadcasted_iota(jnp.int32, (CHUNK, CHUNK), 0)
    ci = lax.broadcasted_iota(jnp.int32, (CHUNK, CHUNK), 1)
    causal = ri >= ci
    strict = ri > ci
    tri = causal.astype(F32)

    ab = ab_ref[...]
    ld = nega_ref[...] * jax.nn.softplus(ab + dtb_ref[...])
    beta_scr[...] = jax.nn.sigmoid(ab)
    for cc in range(rows // CHUNK):
        g = jnp.dot(tri, ld[cc * CHUNK:(cc + 1) * CHUNK], preferred_element_type=F32, precision=HIGHEST)
        g_scr[cc * CHUNK:(cc + 1) * CHUNK, :] = g
        gt_scr[cc] = g.T

    lane = lax.broadcasted_iota(jnp.int32, (CHUNK, LANES), 1)

    hng = hng_ref[...]

    def chunk_head(h, q, k, v, z, S, gch, g_row, btc):
        gb = jnp.broadcast_to(jnp.sum(jnp.where(lane == h, gch, 0.0), axis=-1, keepdims=True),
                              (CHUNK, LANES))
        beta_b = jnp.sum(jnp.where(lane == n_heads + h, btc, 0.0), axis=-1, keepdims=True)
        diff = gb[:, :CHUNK] - g_row
        decay = jnp.where(causal, jnp.exp(jnp.where(causal, diff, 0.0)), 0.0)
        eg = jnp.exp(gb)
        g_last = gb[CHUNK - 1:CHUNK, :]
        ew = jnp.exp(g_last - gb)
        e_last = jnp.exp(g_last)

        kb = k * beta_b
        kq = lax.dot_general(jnp.concatenate([kb, q], axis=0).astype(BF16), k.astype(BF16),
                             (((1,), (1,)), ((), ())), preferred_element_type=F32)
        yield
        L = jnp.where(strict, kq[:CHUNK] * decay, 0.0)
        intra = kq[CHUNK:] * decay
        X = jnp.concatenate([v * beta_b, kb * eg], axis=1)
        X = yield from _solve_unit_lower(L, X, ri, ci)
        yield
        Xb = X.astype(BF16)
        iv = jnp.dot(intra.astype(BF16), Xb, preferred_element_type=F32)
        kv = lax.dot_general((k * ew).astype(BF16), Xb, (((0,), (0,)), ((), ())),
                             preferred_element_type=F32)
        yield
        qe = q * eg - iv[:, HEAD_DIM:]
        lhs = jnp.concatenate([kv[:, HEAD_DIM:], qe], axis=0).astype(BF16)
        rs = jnp.dot(lhs, S.astype(BF16), preferred_element_type=F32)
        yield
        s_new = e_last * S - rs[:HEAD_DIM] + kv[:, :HEAD_DIM]
        o = rs[HEAD_DIM:] + iv[:, :HEAD_DIM]
        o = o * lax.rsqrt(jnp.mean(o * o, axis=-1, keepdims=True) + EPS) * hng
        return s_new, (o * _silu(z)).astype(o_ref.dtype)

    def chunk_body(c, carry):
        rsl = pl.ds(pl.multiple_of(c * CHUNK, CHUNK), CHUNK)
        gch = g_scr[rsl, :]
        btc = beta_scr[rsl, :]
        ins = []
        for j in range(hb):
            sl = slice(j * HEAD_DIM, (j + 1) * HEAD_DIM)
            ins.append((qs_scr[rsl, sl], ks_scr[rsl, sl], vs_scr[rsl, sl],
                        z_ref[rsl, sl].astype(F32), s_scr[j]))
        outs = _round_robin([chunk_head(hg * hb + j, *ins[j], gch, gt_scr[c, pl.ds(hg * hb + j, 1), :], btc)
                             for j in range(hb)])
        for j in range(hb):
            s_scr[j] = outs[j][0]
            o_ref[rsl, j * HEAD_DIM:(j + 1) * HEAD_DIM] = outs[j][1]
        return carry

    lax.fori_loop(0, rows // CHUNK, chunk_body, 0)


def _delta(proj, ab, conv_w, a_log, dt_bias, head_norm_g, *, batch, seq, n_heads, col0):
    T = proj.shape[0]
    KW = n_heads * HEAD_DIM
    hb = next(d for d in (16, 8, 4, 2, 1) if n_heads % d == 0)
    C = hb * HEAD_DIM
    rows = _tile(seq, 256)
    nb = seq // rows
    assert col0 % C == 0 and KW % C == 0 and n_heads <= LANES // 2
    qb, kb_, vb, zb = ((col0 + s * KW) // C for s in range(4))
    row = lambda b, g, c: b * nb + c
    pad = lambda t: jnp.zeros((1, LANES), F32).at[0, :n_heads].set(t)
    nega = pad(-jnp.exp(a_log.astype(F32)))
    dtb = pad(dt_bias.astype(F32))
    wq, wk, wv = (conv_w[:, s * KW:(s + 1) * KW] for s in range(3))
    kern = functools.partial(_delta_kernel, hb=hb, rows=rows, n_heads=n_heads)
    wspec = pl.BlockSpec((CONV_WIDTH, C), lambda b, g, c: (0, g))
    vec = pl.BlockSpec((1, LANES), lambda b, g, c: (0, 0))
    return pl.pallas_call(
        kern,
        grid=(batch, n_heads // hb, nb),
        in_specs=[pl.BlockSpec((rows, C), lambda b, g, c: (row(b, g, c), qb + g)),
                  pl.BlockSpec((rows, C), lambda b, g, c: (row(b, g, c), kb_ + g)),
                  pl.BlockSpec((rows, C), lambda b, g, c: (row(b, g, c), vb + g)),
                  pl.BlockSpec((rows, C), lambda b, g, c: (row(b, g, c), zb + g)),
                  pl.BlockSpec((rows, LANES), lambda b, g, c: (row(b, g, c), 0)),
                  wspec, wspec, wspec, vec, vec, vec],
        out_specs=pl.BlockSpec((rows, C), lambda b, g, c: (row(b, g, c), g)),
        out_shape=jax.ShapeDtypeStruct((T, KW), BF16),
        scratch_shapes=[pltpu.VMEM((hb, HEAD_DIM, HEAD_DIM), F32),
                        pltpu.VMEM((3, 8, C), F32),
                        pltpu.VMEM((rows, C), F32),
                        pltpu.VMEM((rows, C), F32),
                        pltpu.VMEM((rows, C), F32),
                        pltpu.VMEM((rows, LANES), F32),
                        pltpu.VMEM((rows // CHUNK, LANES, CHUNK), F32),
                        pltpu.VMEM((rows, LANES), F32)],
        compiler_params=_params(("parallel", "parallel", "arbitrary")),
        name="delta",
    )(proj, proj, proj, proj, ab, wq, wk, wv, nega, dtb, head_norm_g.reshape(1, HEAD_DIM).astype(F32))


def _merge_kernel(a1_ref, a2_ref, w1_ref, w2_ref, g1_ref, g2_ref, src_ref, o_ref, dst_ref):
    y1 = jnp.dot(a1_ref[...], w1_ref[...], preferred_element_type=F32)
    y2 = jnp.dot(a2_ref[...], w2_ref[...], preferred_element_type=F32)
    m = jax.nn.sigmoid(g1_ref[...].astype(F32)) * y1 + jax.nn.sigmoid(g2_ref[...].astype(F32)) * y2
    o_ref[...] = m.astype(o_ref.dtype)
    dst_ref[...] = src_ref[...].astype(dst_ref.dtype)


def _merge(yp, yd, w1, w2, gates, cast_along):
    T, K1 = yp.shape
    K2 = yd.shape[1]
    D = w1.shape[1]
    tm, tn = _tile(T, 1024), _tile(D, 512)
    ni, nj = T // tm, D // tn
    rows = cast_along.shape[0] // (ni * nj)
    assert rows * ni * nj == cast_along.shape[0] and rows % 16 == 0
    side = pl.BlockSpec((rows, cast_along.shape[1]), lambda i, j: (i * nj + j, 0))
    return pl.pallas_call(
        _merge_kernel,
        grid=(ni, nj),
        in_specs=[pl.BlockSpec((tm, K1), lambda i, j: (i, 0)),
                  pl.BlockSpec((tm, K2), lambda i, j: (i, 0)),
                  pl.BlockSpec((K1, tn), lambda i, j: (0, j)),
                  pl.BlockSpec((K2, tn), lambda i, j: (0, j)),
                  pl.BlockSpec((tm, tn), lambda i, j: (i, j)),
                  pl.BlockSpec((tm, tn), lambda i, j: (i, nj + j)),
                  side],
        out_specs=[pl.BlockSpec((tm, tn), lambda i, j: (i, j)), side],
        out_shape=[jax.ShapeDtypeStruct((T, D), BF16), jax.ShapeDtypeStruct(cast_along.shape, BF16)],
        compiler_params=_params(("arbitrary", "arbitrary")),
        name="merge",
    )(yp, yd, w1, w2, gates, gates, cast_along)


def _outproj_kernel(a_ref, w_ref, x_ref, gate_ref, o_ref):
    acc = jnp.dot(a_ref[...], w_ref[...], preferred_element_type=F32)
    o_ref[...] = x_ref[...] + gate_ref[0] * acc


def _outproj(a, w, x2, gate, seq):
    T, K = a.shape
    D = w.shape[1]
    tm, tn = _tile(seq, 1024), _tile(D, 1024)
    return pl.pallas_call(
        _outproj_kernel,
        grid=(T // tm, D // tn),
        in_specs=[pl.BlockSpec((tm, K), lambda i, j: (i, 0)),
                  pl.BlockSpec((K, tn), lambda i, j: (0, j)),
                  pl.BlockSpec((tm, tn), lambda i, j: (i, j)),
                  pl.BlockSpec((1, 1, tn), lambda i, j: ((i * tm) // seq, 0, j))],
        out_specs=pl.BlockSpec((tm, tn), lambda i, j: (i, j)),
        out_shape=jax.ShapeDtypeStruct((T, D), F32),
        compiler_params=_params(("parallel", "parallel")),
        name="outproj",
    )(a, w, x2, gate)


def _ffn_kernel(u_ref, w1_ref, w2_ref, h_hbm, gate_ref, fg_ref, o_ref, hbuf, hsem, *, nf, tn):
    i = pl.program_id(0)
    f = pl.program_id(1)
    tm = o_ref.shape[0]
    rows = hbuf.shape[1]
    nchunk = tm // rows

    def h_copy(r, slot):
        return pltpu.make_async_copy(h_hbm.at[pl.ds(i * tm + r * rows, rows), :], hbuf.at[slot], hsem.at[slot])

    @pl.when(f == 0)
    def _():
        o_ref[...] = jnp.zeros_like(o_ref)

    @pl.when(f == nf - 1)
    def _():
        h_copy(0, 0).start()

    hdn = jnp.maximum(jnp.dot(u_ref[...], w1_ref[...], preferred_element_type=F32), 0.0)
    hb = (hdn * hdn).astype(BF16)
    for n0 in range(0, o_ref.shape[1], tn):
        o_ref[:, n0:n0 + tn] += jnp.dot(hb, w2_ref[:, n0:n0 + tn], preferred_element_type=F32)

    @pl.when(f == nf - 1)
    def _():
        def body(r, carry):
            slot = lax.rem(r, 2)

            @pl.when(r + 1 < nchunk)
            def _():
                h_copy(r + 1, 1 - slot).start()

            h_copy(r, slot).wait()
            sl = pl.ds(pl.multiple_of(r * rows, rows), rows)
            h2 = hbuf[slot] + gate_ref[0] * o_ref[sl, :]
            o_ref[sl, :] = h2 * lax.rsqrt(jnp.mean(h2 * h2, axis=-1, keepdims=True) + EPS) * fg_ref[...]
            return carry

        lax.fori_loop(0, nchunk, body, 0)


def _ffn(u2, w1, w2, h, gate, final_g, seq):
    T, D = u2.shape
    F = w1.shape[1]
    tm, tf = _tile(seq, FFN_TM), _tile(F, FFN_TF)
    nf = F // tf
    once = dict(pipeline_mode=pl.Buffered(1))
    return pl.pallas_call(
        functools.partial(_ffn_kernel, nf=nf, tn=_tile(D, 512)),
        grid=(T // tm, nf),
        in_specs=[pl.BlockSpec((tm, D), lambda i, f: (i, 0), **once),
                  pl.BlockSpec((D, tf), lambda i, f: (0, f)),
                  pl.BlockSpec((tf, D), lambda i, f: (f, 0)),
                  pl.BlockSpec(memory_space=pl.ANY),
                  pl.BlockSpec((1, 1, D), lambda i, f: ((i * tm) // seq, 0, 0)),
                  pl.BlockSpec((1, D), lambda i, f: (0, 0))],
        out_specs=pl.BlockSpec((tm, D), lambda i, f: (i, 0), **once),
        out_shape=jax.ShapeDtypeStruct((T, D), F32),
        scratch_shapes=[pltpu.VMEM((2, _tile(tm, FFN_EPILOGUE_ROWS), D), F32),
                        pltpu.SemaphoreType.DMA((2,))],
        compiler_params=_params(("arbitrary", "arbitrary")),
        name="ffn",
    )(u2, w1, w2, h, gate, final_g.reshape(1, D))


def kernel(x, c, w_ada, b_ada, norm1_g, w_in, conv_w, pool_w, pool_scale, a_log, dt_bias,
           head_norm_g, w_up_pool, w_up_delta, w_out, norm2_g, w_ff_in, w_ff_out, final_g):
    B, S, D = x.shape
    T = B * S
    H = a_log.shape[0]
    KW = H * HEAD_DIM
    PW = pool_scale.shape[0]
    assert w_in.shape[1] == PW + 3 * KW + KW + 2 * H + 2 * D
    assert w_ada.shape[1] == N_MOD * D and S % CHUNK == 0 and 2 * H <= LANES

    mod = _ada(c, w_ada, b_ada)
    shift1, scale1, gate1, shift2, scale2, gate2 = (
        mod[:, i * D:(i + 1) * D].reshape(B, 1, D) for i in range(N_MOD))

    c_ab = PW + 4 * KW
    w_proj = w_in[:, :c_ab].astype(BF16)
    w_gates = w_in[:, c_ab + 2 * H:].astype(BF16)
    w_ab = jnp.pad(w_in[:, c_ab:c_ab + 2 * H], ((0, 0), (0, LANES - 2 * H))).astype(BF16)

    x2 = x.reshape(T, D)
    u = _norm_mod(x2, norm1_g, shift1, scale1, S)
    proj = _matmul(u, w_proj, BF16, name="inproj")
    G, Cg, _ = pool_w.shape
    gates, (w_up_pool, w_up_delta, w_out, w_ff_in, pool_w) = _matmul(
        u, w_gates, BF16, name="inproj_gates",
        cast_along=(w_up_pool, w_up_delta, w_out, w_ff_in, pool_w.reshape(G * Cg, Cg)))
    ab = _matmul(u, w_ab, F32, name="inproj_ab")

    yp = _pool(proj, pool_w.reshape(G, Cg, Cg), pool_scale, S)
    yd = _delta(proj, ab, conv_w, a_log, dt_bias, head_norm_g,
                batch=B, seq=S, n_heads=H, col0=PW)
    merged, w_ff_out = _merge(yp, yd, w_up_pool, w_up_delta, gates, cast_along=w_ff_out)
    h = _outproj(merged, w_out, x2, gate1, S)

    u2 = _norm_mod(h, norm2_g, shift2, scale2, S)
    out = _ffn(u2, w_ff_in, w_ff_out, h, gate2, final_g, S)
    return out.reshape(B, S, D)
```

```python
import functools

import jax
import jax.numpy as jnp
from jax import lax
from jax.experimental import pallas as pl
from jax.experimental.pallas import tpu as pltpu

EPS = 1e-6
HEAD_DIM = 128
CHUNK = 64
CONV_WIDTH = 4
POOL_WINDOWS = (2, 4, 8, 16)
N_MOD = 6
LANES = 128
VMEM_LIMIT = 56 * 1024 * 1024
FFN_TM, FFN_TF = 512, 1024
FFN_EPILOGUE_ROWS = 128
F32 = jnp.float32
BF16 = jnp.bfloat16
HIGHEST = lax.Precision.HIGHEST


def _params(sem):
    return pltpu.CompilerParams(dimension_semantics=sem, vmem_limit_bytes=VMEM_LIMIT)


def _tile(n, pref):
    t = min(n, pref)
    while n % t:
        t //= 2
    return t


def _silu(x):
    return x * jax.nn.sigmoid(x)


def _cast_cols_kernel(a_ref, b_ref, o_ref, *, shift):
    if shift:
        x = jnp.concatenate([a_ref[...], b_ref[...]], axis=1)
        x = pltpu.roll(x, x.shape[1] - shift, axis=1)[:, :o_ref.shape[1]]
    else:
        x = a_ref[...]
    o_ref[...] = x.astype(o_ref.dtype)


def _cast_cols(w, col0, ncols):
    K = w.shape[0]
    shift = col0 % LANES
    base = col0 - shift
    tn = next(t for t in (2048, 1024, 512, 256, 128) if base % t == 0 and ncols % t == 0)
    rb = _tile(K, 1024)
    a0, b0 = base // tn, base // LANES
    b_last = pl.cdiv(w.shape[1], LANES) - 1
    return pl.pallas_call(
        functools.partial(_cast_cols_kernel, shift=shift),
        grid=(K // rb, ncols // tn),
        in_specs=[pl.BlockSpec((rb, tn), lambda i, j: (i, a0 + j)),
                  pl.BlockSpec((rb, LANES),
                               lambda i, j: (i, jnp.minimum(b0 + (j + 1) * (tn // LANES), b_last)))],
        out_specs=pl.BlockSpec((rb, tn), lambda i, j: (i, j)),
        out_shape=jax.ShapeDtypeStruct((K, ncols), BF16),
        compiler_params=_params(("parallel", "parallel")),
        name="cast_cols",
    )(w, w)


def _ada_kernel(c_ref, w_ref, b_ref, o_ref):
    s = _silu(c_ref[...]).astype(BF16)
    o_ref[...] = jnp.dot(s, w_ref[...].astype(BF16), preferred_element_type=F32) + b_ref[...]


def _ada(c, w_ada, b_ada):
    B, D = c.shape
    N = w_ada.shape[1]
    rows = 8
    cp = jnp.zeros((rows, D), F32).at[:B].set(c)
    tn = _tile(N, 512)
    out = pl.pallas_call(
        _ada_kernel,
        grid=(N // tn,),
        in_specs=[pl.BlockSpec((rows, D), lambda j: (0, 0)),
                  pl.BlockSpec((D, tn), lambda j: (0, j)),
                  pl.BlockSpec((1, tn), lambda j: (0, j))],
        out_specs=pl.BlockSpec((rows, tn), lambda j: (0, j)),
        out_shape=jax.ShapeDtypeStruct((rows, N), F32),
        compiler_params=_params(("parallel",)),
        name="ada",
    )(cp, w_ada, b_ada.reshape(1, N))
    return out[:B]


def _norm_kernel(x_ref, g_ref, shift_ref, scale_ref, o_ref):
    x = x_ref[...]
    n = x * lax.rsqrt(jnp.mean(x * x, axis=-1, keepdims=True) + EPS) * g_ref[...]
    o_ref[...] = (n * (1.0 + scale_ref[0]) + shift_ref[0]).astype(o_ref.dtype)


def _norm_mod(x2, g, shift, scale, seq):
    T, D = x2.shape
    tm = _tile(seq, 512)
    bmap = lambda i: ((i * tm) // seq, 0, 0)
    return pl.pallas_call(
        _norm_kernel,
        grid=(T // tm,),
        in_specs=[pl.BlockSpec((tm, D), lambda i: (i, 0)),
                  pl.BlockSpec((1, D), lambda i: (0, 0)),
                  pl.BlockSpec((1, 1, D), bmap),
                  pl.BlockSpec((1, 1, D), bmap)],
        out_specs=pl.BlockSpec((tm, D), lambda i: (i, 0)),
        out_shape=jax.ShapeDtypeStruct((T, D), BF16),
        compiler_params=_params(("parallel",)),
        name="norm_mod",
    )(x2, g.reshape(1, D), shift, scale)


def _mm_kernel(a_ref, w_ref, *refs):
    n_side = (len(refs) - 1) // 2
    o_ref = refs[n_side]
    o_ref[...] = jnp.dot(a_ref[...], w_ref[...], preferred_element_type=F32).astype(o_ref.dtype)
    for src, dst in zip(refs[:n_side], refs[n_side + 1:]):
        dst[...] = src[...].astype(dst.dtype)


def _matmul(a, w, out_dtype, tm_pref=1024, tn_pref=1024, name="mm", cast_along=()):
    M, K = a.shape
    N = w.shape[1]
    tm, tn = _tile(M, tm_pref), _tile(N, tn_pref)
    ni, nj = M // tm, N // tn
    steps = ni * nj
    side_specs = []
    for t in cast_along:
        rows = t.shape[0] // steps
        assert rows * steps == t.shape[0] and rows % 16 == 0, (t.shape, steps)
        side_specs.append(pl.BlockSpec((rows, t.shape[1]), lambda i, j: (i * nj + j, 0)))
    outs = pl.pallas_call(
        _mm_kernel,
        grid=(ni, nj),
        in_specs=[pl.BlockSpec((tm, K), lambda i, j: (i, 0)),
                  pl.BlockSpec((K, tn), lambda i, j: (0, j))] + side_specs,
        out_specs=[pl.BlockSpec((tm, tn), lambda i, j: (i, j))] + side_specs,
        out_shape=[jax.ShapeDtypeStruct((M, N), out_dtype)]
                  + [jax.ShapeDtypeStruct(t.shape, BF16) for t in cast_along],
        compiler_params=_params(("arbitrary", "arbitrary")),
        name=name,
    )(a, w, *cast_along)
    return (outs[0], outs[1:]) if cast_along else outs[0]


def _pool_kernel(p_ref, halo_ref, w_ref, scale_ref, o_ref, *, seq, tm, halo):
    i = pl.program_id(0)
    g = pl.program_id(1)
    start = (i * tm) % seq
    p = p_ref[...]
    hl = jnp.where(start > 0, halo_ref[...], jnp.zeros_like(halo_ref))
    ext = jnp.concatenate([jnp.zeros((LANES - halo, p.shape[1]), p.dtype), hl, p], axis=0)
    win = jnp.where(g == 0, POOL_WINDOWS[0],
                    jnp.where(g == 1, POOL_WINDOWS[1],
                              jnp.where(g == 2, POOL_WINDOWS[2], POOL_WINDOWS[3])))
    r = lax.broadcasted_iota(jnp.int32, (tm, tm + LANES), 0)
    cidx = lax.broadcasted_iota(jnp.int32, (tm, tm + LANES), 1)
    d = r + LANES - cidx
    band = jnp.where((d >= 0) & (d < win), 1.0, 0.0).astype(BF16)
    wsum = jnp.dot(band, ext, preferred_element_type=F32)
    pos = start + lax.broadcasted_iota(jnp.int32, (tm, 1), 0) + 1
    count = jnp.minimum(pos, win).astype(F32)
    mixed = wsum / count - p.astype(F32)
    y = jnp.dot(mixed.astype(BF16), w_ref[0], preferred_element_type=F32)
    o_ref[...] = (y * scale_ref[...]).astype(o_ref.dtype)


def _pool(proj, pool_w_bf, pool_scale, seq):
    T = proj.shape[0]
    G, Cg, _ = pool_w_bf.shape
    assert G == len(POOL_WINDOWS)
    halo = 16
    tm = _tile(seq, 512)
    hb = tm // halo
    kern = functools.partial(_pool_kernel, seq=seq, tm=tm, halo=halo)
    return pl.pallas_call(
        kern,
        grid=(T // tm, G),
        in_specs=[pl.BlockSpec((tm, Cg), lambda i, g: (i, g)),
                  pl.BlockSpec((halo, Cg), lambda i, g: (jnp.maximum(i * hb - 1, 0), g)),
                  pl.BlockSpec((1, Cg, Cg), lambda i, g: (g, 0, 0)),
                  pl.BlockSpec((1, Cg), lambda i, g: (0, g))],
        out_specs=pl.BlockSpec((tm, Cg), lambda i, g: (i, g)),
        out_shape=jax.ShapeDtypeStruct((T, G * Cg), BF16),
        compiler_params=_params(("parallel", "parallel")),
        name="pool",
    )(proj, proj, pool_w_bf, pool_scale.reshape(1, G * Cg))


SOLVE_BASE_LOG2 = 3


def _bdot(a, b):
    return jnp.dot(a.astype(BF16), b.astype(BF16), preferred_element_type=F32)


def _solve_unit_lower(L, X, ri, ci):
    b = SOLVE_BASE_LOG2
    A = jnp.where((ri >> b) == (ci >> b), L, 0.0)
    A2 = _bdot(A, A)
    yield
    A3 = _bdot(A, A2)
    A4 = _bdot(A2, A2)
    yield
    T0 = A2 - A - A3
    Tm = T0 + A4 + _bdot(T0, A4)
    yield
    lvl = b
    while (1 << lvl) < CHUNK:
        Cm = jnp.where(((ri >> (lvl + 1)) == (ci >> (lvl + 1))) & ((ri >> lvl) != (ci >> lvl)), L, 0.0)
        M = Cm + _bdot(Cm, Tm)
        yield
        Tm = Tm - M - _bdot(Tm, M)
        yield
        lvl += 1
    return X + _bdot(Tm, X)


def _round_robin(gens):
    out = [None] * len(gens)
    live = list(range(len(gens)))
    while live:
        for j in list(live):
            try:
                next(gens[j])
            except StopIteration as e:
                out[j] = e.value
                live.remove(j)
    return out


def _delta_kernel(q_ref, k_ref, v_ref, z_ref, ab_ref, wq_ref, wk_ref, wv_ref, nega_ref, dtb_ref,
                  hng_ref, o_ref, s_scr, halo_scr, qs_scr, ks_scr, vs_scr, g_scr, gt_scr, beta_scr,
                  *, hb, rows, n_heads):
    hg = pl.program_id(1)
    cb = pl.program_id(2)
    C = hb * HEAD_DIM

    @pl.when(cb == 0)
    def _():
        s_scr[...] = jnp.zeros_like(s_scr)
        halo_scr[...] = jnp.zeros_like(halo_scr)

    def conv(idx, x_ref, w_ref):
        x = x_ref[...].astype(F32)
        ext = jnp.concatenate([halo_scr[idx], x], axis=0)
        w = w_ref[...]
        y = x * w[CONV_WIDTH - 1:CONV_WIDTH]
        for s in range(1, CONV_WIDTH):
            y = y + ext[8 - s:8 - s + rows] * w[CONV_WIDTH - 1 - s:CONV_WIDTH - s]
        halo_scr[idx] = x[rows - 8:rows]
        return _silu(y)

    def l2n(t, mult):
        outs = []
        for j in range(hb):
            th = t[:, j * HEAD_DIM:(j + 1) * HEAD_DIM]
            outs.append(th * (lax.rsqrt(jnp.sum(th * th, axis=-1, keepdims=True) + EPS) * mult))
        return jnp.concatenate(outs, axis=1) if hb > 1 else outs[0]

    qs_scr[...] = l2n(conv(0, q_ref, wq_ref), HEAD_DIM ** -0.5)
    ks_scr[...] = l2n(conv(1, k_ref, wk_ref), 1.0)
    vs_scr[...] = conv(2, v_ref, wv_ref)

    ri = lax.broadcasted_iota(jnp.int32, (CHUNK, CHUNK), 0)
    ci = lax.broadcasted_iota(jnp.int32, (CHUNK, CHUNK), 1)
    causal = ri >= ci
    strict = ri > ci
    tri = causal.astype(F32)

    ab = ab_ref[...]
    ld = nega_ref[...] * jax.nn.softplus(ab + dtb_ref[...])
    beta_scr[...] = jax.nn.sigmoid(ab)
    for cc in range(rows // CHUNK):
        g = jnp.dot(tri, ld[cc * CHUNK:(cc + 1) * CHUNK], preferred_element_type=F32, precision=HIGHEST)
        g_scr[cc * CHUNK:(cc + 1) * CHUNK, :] = g
        gt_scr[cc] = g.T

    lane = lax.broadcasted_iota(jnp.int32, (CHUNK, LANES), 1)

    hng = hng_ref[...]

    def chunk_head(h, q, k, v, z, S, gch, g_row, btc):
        gb = jnp.broadcast_to(jnp.sum(jnp.where(lane == h, gch, 0.0), axis=-1, keepdims=True),
                              (CHUNK, LANES))
        beta_b = jnp.sum(jnp.where(lane == n_heads + h, btc, 0.0), axis=-1, keepdims=True)
        diff = gb[:, :CHUNK] - g_row
        decay = jnp.where(causal, jnp.exp(jnp.where(causal, diff, 0.0)), 0.0)
        eg = jnp.exp(gb)
        g_last = gb[CHUNK - 1:CHUNK, :]
        ew = jnp.exp(g_last - gb)
        e_last = jnp.exp(g_last)

        kb = k * beta_b
        kq = lax.dot_general(jnp.concatenate([kb, q], axis=0).astype(BF16), k.astype(BF16),
                             (((1,), (1,)), ((), ())), preferred_element_type=F32)
        yield
        L = jnp.where(strict, kq[:CHUNK] * decay, 0.0)
        intra = kq[CHUNK:] * decay
        X = jnp.concatenate([v * beta_b, kb * eg], axis=1)
        X = yield from _solve_unit_lower(L, X, ri, ci)
        yield
        Xb = X.astype(BF16)
        iv = jnp.dot(intra.astype(BF16), Xb, preferred_element_type=F32)
        kv = lax.dot_general((k * ew).astype(BF16), Xb, (((0,), (0,)), ((), ())),
                             preferred_element_type=F32)
        yield
        qe = q * eg - iv[:, HEAD_DIM:]
        lhs = jnp.concatenate([kv[:, HEAD_DIM:], qe], axis=0).astype(BF16)
        rs = jnp.dot(lhs, S.astype(BF16), preferred_element_type=F32)
        yield
        s_new = e_last * S - rs[:HEAD_DIM] + kv[:, :HEAD_DIM]
        o = rs[HEAD_DIM:] + iv[:, :HEAD_DIM]
        o = o * lax.rsqrt(jnp.mean(o * o, axis=-1, keepdims=True) + EPS) * hng
        return s_new, (o * _silu(z)).astype(o_ref.dtype)

    def chunk_body(c, carry):
        rsl = pl.ds(pl.multiple_of(c * CHUNK, CHUNK), CHUNK)
        gch = g_scr[rsl, :]
        btc = beta_scr[rsl, :]
        ins = []
        for j in range(hb):
            sl = slice(j * HEAD_DIM, (j + 1) * HEAD_DIM)
            ins.append((qs_scr[rsl, sl], ks_scr[rsl, sl], vs_scr[rsl, sl],
                        z_ref[rsl, sl].astype(F32), s_scr[j]))
        outs = _round_robin([chunk_head(hg * hb + j, *ins[j], gch, gt_scr[c, pl.ds(hg * hb + j, 1), :], btc)
                             for j in range(hb)])
        for j in range(hb):
            s_scr[j] = outs[j][0]
            o_ref[rsl, j * HEAD_DIM:(j + 1) * HEAD_DIM] = outs[j][1]
        return carry

    lax.fori_loop(0, rows // CHUNK, chunk_body, 0)


def _delta(proj, ab, conv_w, a_log, dt_bias, head_norm_g, *, batch, seq, n_heads, col0):
    T = proj.shape[0]
    KW = n_heads * HEAD_DIM
    hb = next(d for d in (16, 8, 4, 2, 1) if n_heads % d == 0)
    C = hb * HEAD_DIM
    rows = _tile(seq, 256)
    nb = seq // rows
    assert col0 % C == 0 and KW % C == 0 and n_heads <= LANES // 2
    qb, kb_, vb, zb = ((col0 + s * KW) // C for s in range(4))
    row = lambda b, g, c: b * nb + c
    pad = lambda t: jnp.zeros((1, LANES), F32).at[0, :n_heads].set(t)
    nega = pad(-jnp.exp(a_log.astype(F32)))
    dtb = pad(dt_bias.astype(F32))
    wq, wk, wv = (conv_w[:, s * KW:(s + 1) * KW] for s in range(3))
    kern = functools.partial(_delta_kernel, hb=hb, rows=rows, n_heads=n_heads)
    wspec = pl.BlockSpec((CONV_WIDTH, C), lambda b, g, c: (0, g))
    vec = pl.BlockSpec((1, LANES), lambda b, g, c: (0, 0))
    return pl.pallas_call(
        kern,
        grid=(batch, n_heads // hb, nb),
        in_specs=[pl.BlockSpec((rows, C), lambda b, g, c: (row(b, g, c), qb + g)),
                  pl.BlockSpec((rows, C), lambda b, g, c: (row(b, g, c), kb_ + g)),
                  pl.BlockSpec((rows, C), lambda b, g, c: (row(b, g, c), vb + g)),
                  pl.BlockSpec((rows, C), lambda b, g, c: (row(b, g, c), zb + g)),
                  pl.BlockSpec((rows, LANES), lambda b, g, c: (row(b, g, c), 0)),
                  wspec, wspec, wspec, vec, vec, vec],
        out_specs=pl.BlockSpec((rows, C), lambda b, g, c: (row(b, g, c), g)),
        out_shape=jax.ShapeDtypeStruct((T, KW), BF16),
        scratch_shapes=[pltpu.VMEM((hb, HEAD_DIM, HEAD_DIM), F32),
                        pltpu.VMEM((3, 8, C), F32),
                        pltpu.VMEM((rows, C), F32),
                        pltpu.VMEM((rows, C), F32),
                        pltpu.VMEM((rows, C), F32),
                        pltpu.VMEM((rows, LANES), F32),
                        pltpu.VMEM((rows // CHUNK, LANES, CHUNK), F32),
                        pltpu.VMEM((rows, LANES), F32)],
        compiler_params=_params(("parallel", "parallel", "arbitrary")),
        name="delta",
    )(proj, proj, proj, proj, ab, wq, wk, wv, nega, dtb, head_norm_g.reshape(1, HEAD_DIM).astype(F32))


def _merge_kernel(a1_ref, a2_ref, w1_ref, w2_ref, g1_ref, g2_ref, src_ref, o_ref, dst_ref):
    y1 = jnp.dot(a1_ref[...], w1_ref[...], preferred_element_type=F32)
    y2 = jnp.dot(a2_ref[...], w2_ref[...], preferred_element_type=F32)
    m = jax.nn.sigmoid(g1_ref[...].astype(F32)) * y1 + jax.nn.sigmoid(g2_ref[...].astype(F32)) * y2
    o_ref[...] = m.astype(o_ref.dtype)
    dst_ref[...] = src_ref[...].astype(dst_ref.dtype)


def _merge(yp, yd, w1, w2, gates, cast_along):
    T, K1 = yp.shape
    K2 = yd.shape[1]
    D = w1.shape[1]
    tm, tn = _tile(T, 1024), _tile(D, 512)
    ni, nj = T // tm, D // tn
    rows = cast_along.shape[0] // (ni * nj)
    assert rows * ni * nj == cast_along.shape[0] and rows % 16 == 0
    side = pl.BlockSpec((rows, cast_along.shape[1]), lambda i, j: (i * nj + j, 0))
    return pl.pallas_call(
        _merge_kernel,
        grid=(ni, nj),
        in_specs=[pl.BlockSpec((tm, K1), lambda i, j: (i, 0)),
                  pl.BlockSpec((tm, K2), lambda i, j: (i, 0)),
                  pl.BlockSpec((K1, tn), lambda i, j: (0, j)),
                  pl.BlockSpec((K2, tn), lambda i, j: (0, j)),
                  pl.BlockSpec((tm, tn), lambda i, j: (i, j)),
                  pl.BlockSpec((tm, tn), lambda i, j: (i, nj + j)),
                  side],
        out_specs=[pl.BlockSpec((tm, tn), lambda i, j: (i, j)), side],
        out_shape=[jax.ShapeDtypeStruct((T, D), BF16), jax.ShapeDtypeStruct(cast_along.shape, BF16)],
        compiler_params=_params(("arbitrary", "arbitrary")),
        name="merge",
    )(yp, yd, w1, w2, gates, gates, cast_along)


def _outproj_kernel(a_ref, w_ref, x_ref, gate_ref, o_ref):
    acc = jnp.dot(a_ref[...], w_ref[...], preferred_element_type=F32)
    o_ref[...] = x_ref[...] + gate_ref[0] * acc


def _outproj(a, w, x2, gate, seq):
    T, K = a.shape
    D = w.shape[1]
    tm, tn = _tile(seq, 1024), _tile(D, 1024)
    return pl.pallas_call(
        _outproj_kernel,
        grid=(T // tm, D // tn),
        in_specs=[pl.BlockSpec((tm, K), lambda i, j: (i, 0)),
                  pl.BlockSpec((K, tn), lambda i, j: (0, j)),
                  pl.BlockSpec((tm, tn), lambda i, j: (i, j)),
                  pl.BlockSpec((1, 1, tn), lambda i, j: ((i * tm) // seq, 0, j))],
        out_specs=pl.BlockSpec((tm, tn), lambda i, j: (i, j)),
        out_shape=jax.ShapeDtypeStruct((T, D), F32),
        compiler_params=_params(("parallel", "parallel")),
        name="outproj",
    )(a, w, x2, gate)


def _ffn_kernel(u_ref, w1_ref, w2_ref, h_hbm, gate_ref, fg_ref, o_ref, hbuf, hsem, *, nf, tn):
    i = pl.program_id(0)
    f = pl.program_id(1)
    tm = o_ref.shape[0]
    rows = hbuf.shape[1]
    nchunk = tm // rows

    def h_copy(r, slot):
        return pltpu.make_async_copy(h_hbm.at[pl.ds(i * tm + r * rows, rows), :], hbuf.at[slot], hsem.at[slot])

    @pl.when(f == 0)
    def _():
        o_ref[...] = jnp.zeros_like(o_ref)

    @pl.when(f == nf - 1)
    def _():
        h_copy(0, 0).start()

    hdn = jnp.maximum(jnp.dot(u_ref[...], w1_ref[...], preferred_element_type=F32), 0.0)
    hb = (hdn * hdn).astype(BF16)
    for n0 in range(0, o_ref.shape[1], tn):
        o_ref[:, n0:n0 + tn] += jnp.dot(hb, w2_ref[:, n0:n0 + tn], preferred_element_type=F32)

    @pl.when(f == nf - 1)
    def _():
        def body(r, carry):
            slot = lax.rem(r, 2)

            @pl.when(r + 1 < nchunk)
            def _():
                h_copy(r + 1, 1 - slot).start()

            h_copy(r, slot).wait()
            sl = pl.ds(pl.multiple_of(r * rows, rows), rows)
            h2 = hbuf[slot] + gate_ref[0] * o_ref[sl, :]
            o_ref[sl, :] = h2 * lax.rsqrt(jnp.mean(h2 * h2, axis=-1, keepdims=True) + EPS) * fg_ref[...]
            return carry

        lax.fori_loop(0, nchunk, body, 0)


def _ffn(u2, w1, w2, h, gate, final_g, seq):
    T, D = u2.shape
    F = w1.shape[1]
    tm, tf = _tile(seq, FFN_TM), _tile(F, FFN_TF)
    nf = F // tf
    once = dict(pipeline_mode=pl.Buffered(1))
    return pl.pallas_call(
        functools.partial(_ffn_kernel, nf=nf, tn=_tile(D, 512)),
        grid=(T // tm, nf),
        in_specs=[pl.BlockSpec((tm, D), lambda i, f: (i, 0), **once),
                  pl.BlockSpec((D, tf), lambda i, f: (0, f)),
                  pl.BlockSpec((tf, D), lambda i, f: (f, 0)),
                  pl.BlockSpec(memory_space=pl.ANY),
                  pl.BlockSpec((1, 1, D), lambda i, f: ((i * tm) // seq, 0, 0)),
                  pl.BlockSpec((1, D), lambda i, f: (0, 0))],
        out_specs=pl.BlockSpec((tm, D), lambda i, f: (i, 0), **once),
        out_shape=jax.ShapeDtypeStruct((T, D), F32),
        scratch_shapes=[pltpu.VMEM((2, _tile(tm, FFN_EPILOGUE_ROWS), D), F32),
                        pltpu.SemaphoreType.DMA((2,))],
        compiler_params=_params(("arbitrary", "arbitrary")),
        name="ffn",
    )(u2, w1, w2, h, gate, final_g.reshape(1, D))


def kernel(x, c, w_ada, b_ada, norm1_g, w_in, conv_w, pool_w, pool_scale, a_log, dt_bias,
           head_norm_g, w_up_pool, w_up_delta, w_out, norm2_g, w_ff_in, w_ff_out, final_g):
    B, S, D = x.shape
    T = B * S
    H = a_log.shape[0]
    KW = H * HEAD_DIM
    PW = pool_scale.shape[0]
    assert w_in.shape[1] == PW + 3 * KW + KW + 2 * H + 2 * D
    assert w_ada.shape[1] == N_MOD * D and S % CHUNK == 0 and 2 * H <= LANES

    mod = _ada(c, w_ada, b_ada)
    shift1, scale1, gate1, shift2, scale2, gate2 = (
        mod[:, i * D:(i + 1) * D].reshape(B, 1, D) for i in range(N_MOD))

    c_ab = PW + 4 * KW
    w_proj = _cast_cols(w_in, 0, c_ab)
    w_gates = _cast_cols(w_in, c_ab + 2 * H, 2 * D)
    w_ab = jnp.pad(w_in[:, c_ab:c_ab + 2 * H], ((0, 0), (0, LANES - 2 * H))).astype(BF16)

    x2 = x.reshape(T, D)
    u = _norm_mod(x2, norm1_g, shift1, scale1, S)
    proj = _matmul(u, w_proj, BF16, name="inproj")
    G, Cg, _ = pool_w.shape
    gates, (w_up_pool, w_up_delta, w_out, w_ff_in, pool_w) = _matmul(
        u, w_gates, BF16, name="inproj_gates",
        cast_along=(w_up_pool, w_up_delta, w_out, w_ff_in, pool_w.reshape(G * Cg, Cg)))
    ab = _matmul(u, w_ab, F32, name="inproj_ab")

    yp = _pool(proj, pool_w.reshape(G, Cg, Cg), pool_scale, S)
    yd = _delta(proj, ab, conv_w, a_log, dt_bias, head_norm_g,
                batch=B, seq=S, n_heads=H, col0=PW)
    merged, w_ff_out = _merge(yp, yd, w_up_pool, w_up_delta, gates, cast_along=w_ff_out)
    h = _outproj(merged, w_out, x2, gate1, S)

    u2 = _norm_mod(h, norm2_g, shift2, scale2, S)
    out = _ffn(u2, w_ff_in, w_ff_out, h, gate2, final_g, S)
    return out.reshape(B, S, D)
```

```python
import functools

import jax
import jax.numpy as jnp
from jax import lax
from jax.experimental import pallas as pl
from jax.experimental.pallas import tpu as pltpu

EPS = 1e-6
HEAD_DIM = 128
CHUNK = 64
CONV_WIDTH = 4
POOL_WINDOWS = (2, 4, 8, 16)
N_MOD = 6
LANES = 128
VMEM_LIMIT = 56 * 1024 * 1024
FFN_TM, FFN_TF = 1024, 512
FFN_EPILOGUE_ROWS = 128
F32 = jnp.float32
BF16 = jnp.bfloat16
HIGHEST = lax.Precision.HIGHEST


def _params(sem):
    return pltpu.CompilerParams(dimension_semantics=sem, vmem_limit_bytes=VMEM_LIMIT)


def _tile(n, pref):
    t = min(n, pref)
    while n % t:
        t //= 2
    return t


def _silu(x):
    return x * jax.nn.sigmoid(x)


def _ada_kernel(c_ref, w_ref, b_ref, o_ref):
    s = _silu(c_ref[...]).astype(BF16)
    o_ref[...] = jnp.dot(s, w_ref[...].astype(BF16), preferred_element_type=F32) + b_ref[...]


def _ada(c, w_ada, b_ada):
    B, D = c.shape
    N = w_ada.shape[1]
    rows = 8
    cp = jnp.zeros((rows, D), F32).at[:B].set(c)
    tn = _tile(N, 512)
    out = pl.pallas_call(
        _ada_kernel,
        grid=(N // tn,),
        in_specs=[pl.BlockSpec((rows, D), lambda j: (0, 0)),
                  pl.BlockSpec((D, tn), lambda j: (0, j)),
                  pl.BlockSpec((1, tn), lambda j: (0, j))],
        out_specs=pl.BlockSpec((rows, tn), lambda j: (0, j)),
        out_shape=jax.ShapeDtypeStruct((rows, N), F32),
        compiler_params=_params(("parallel",)),
        name="ada",
    )(cp, w_ada, b_ada.reshape(1, N))
    return out[:B]


def _norm_kernel(x_ref, g_ref, shift_ref, scale_ref, o_ref):
    x = x_ref[...]
    n = x * lax.rsqrt(jnp.mean(x * x, axis=-1, keepdims=True) + EPS) * g_ref[...]
    o_ref[...] = (n * (1.0 + scale_ref[0]) + shift_ref[0]).astype(o_ref.dtype)


def _norm_mod(x2, g, shift, scale, seq):
    T, D = x2.shape
    tm = _tile(seq, 512)
    bmap = lambda i: ((i * tm) // seq, 0, 0)
    return pl.pallas_call(
        _norm_kernel,
        grid=(T // tm,),
        in_specs=[pl.BlockSpec((tm, D), lambda i: (i, 0)),
                  pl.BlockSpec((1, D), lambda i: (0, 0)),
                  pl.BlockSpec((1, 1, D), bmap),
                  pl.BlockSpec((1, 1, D), bmap)],
        out_specs=pl.BlockSpec((tm, D), lambda i: (i, 0)),
        out_shape=jax.ShapeDtypeStruct((T, D), BF16),
        compiler_params=_params(("parallel",)),
        name="norm_mod",
    )(x2, g.reshape(1, D), shift, scale)


def _mm_kernel(a_ref, w_ref, *refs, n_narrow):
    n_side = (len(refs) - 1 - 2 * n_narrow) // 2
    ins, outs = refs[:n_narrow + n_side], refs[n_narrow + n_side:]
    o_ref = outs[0]
    o_ref[...] = jnp.dot(a_ref[...], w_ref[...], preferred_element_type=F32).astype(o_ref.dtype)
    if n_narrow:
        @pl.when(pl.program_id(1) == 0)
        def _():
            outs[1][...] = jnp.dot(a_ref[...], ins[0][...], preferred_element_type=F32)
    for src, dst in zip(ins[n_narrow:], outs[1 + n_narrow:]):
        dst[...] = src[...].astype(dst.dtype)


def _matmul(a, w, out_dtype, n_cols=None, tm_pref=1024, tn_pref=1024, name="mm", narrow=None, cast_along=()):
    M, K = a.shape
    N = w.shape[1] if n_cols is None else n_cols
    tm, tn = _tile(M, tm_pref), _tile(N, tn_pref)
    ni, nj = M // tm, N // tn
    steps = ni * nj
    side_specs = []
    for t in cast_along:
        rows = t.shape[0] // steps
        assert rows * steps == t.shape[0] and rows % 16 == 0, (t.shape, steps)
        side_specs.append(pl.BlockSpec((rows, t.shape[1]), lambda i, j: (i * nj + j, 0)))
    nar_in = [] if narrow is None else [pl.BlockSpec(narrow.shape, lambda i, j: (0, 0))]
    nar_out = [] if narrow is None else [pl.BlockSpec((tm, narrow.shape[1]), lambda i, j: (i, 0))]
    nar_shape = [] if narrow is None else [jax.ShapeDtypeStruct((M, narrow.shape[1]), F32)]
    outs = pl.pallas_call(
        functools.partial(_mm_kernel, n_narrow=len(nar_in)),
        grid=(ni, nj),
        in_specs=[pl.BlockSpec((tm, K), lambda i, j: (i, 0)),
                  pl.BlockSpec((K, tn), lambda i, j: (0, j))] + nar_in + side_specs,
        out_specs=[pl.BlockSpec((tm, tn), lambda i, j: (i, j))] + nar_out + side_specs,
        out_shape=[jax.ShapeDtypeStruct((M, N), out_dtype)] + nar_shape
                  + [jax.ShapeDtypeStruct(t.shape, BF16) for t in cast_along],
        compiler_params=_params(("arbitrary", "arbitrary")),
        name=name,
    )(a, w, *([] if narrow is None else [narrow]), *cast_along)
    return outs[0], (None if narrow is None else outs[1]), outs[1 + len(nar_in):]


def _pool_kernel(p_ref, halo_ref, w_ref, scale_ref, o_ref, *, seq, tm, halo):
    i = pl.program_id(0)
    g = pl.program_id(1)
    start = (i * tm) % seq
    p = p_ref[...]
    hl = jnp.where(start > 0, halo_ref[...], jnp.zeros_like(halo_ref))
    ext = jnp.concatenate([jnp.zeros((LANES - halo, p.shape[1]), p.dtype), hl, p], axis=0)
    win = jnp.where(g == 0, POOL_WINDOWS[0],
                    jnp.where(g == 1, POOL_WINDOWS[1],
                              jnp.where(g == 2, POOL_WINDOWS[2], POOL_WINDOWS[3])))
    r = lax.broadcasted_iota(jnp.int32, (tm, tm + LANES), 0)
    cidx = lax.broadcasted_iota(jnp.int32, (tm, tm + LANES), 1)
    d = r + LANES - cidx
    band = jnp.where((d >= 0) & (d < win), 1.0, 0.0).astype(BF16)
    wsum = jnp.dot(band, ext, preferred_element_type=F32)
    pos = start + lax.broadcasted_iota(jnp.int32, (tm, 1), 0) + 1
    count = jnp.minimum(pos, win).astype(F32)
    mixed = wsum / count - p.astype(F32)
    y = jnp.dot(mixed.astype(BF16), w_ref[0], preferred_element_type=F32)
    o_ref[...] = (y * scale_ref[...]).astype(o_ref.dtype)


def _pool(proj, pool_w_bf, pool_scale, seq):
    T = proj.shape[0]
    G, Cg, _ = pool_w_bf.shape
    assert G == len(POOL_WINDOWS)
    halo = 16
    tm = _tile(seq, 512)
    hb = tm // halo
    kern = functools.partial(_pool_kernel, seq=seq, tm=tm, halo=halo)
    return pl.pallas_call(
        kern,
        grid=(T // tm, G),
        in_specs=[pl.BlockSpec((tm, Cg), lambda i, g: (i, g)),
                  pl.BlockSpec((halo, Cg), lambda i, g: (jnp.maximum(i * hb - 1, 0), g)),
                  pl.BlockSpec((1, Cg, Cg), lambda i, g: (g, 0, 0)),
                  pl.BlockSpec((1, Cg), lambda i, g: (0, g))],
        out_specs=pl.BlockSpec((tm, Cg), lambda i, g: (i, g)),
        out_shape=jax.ShapeDtypeStruct((T, G * Cg), BF16),
        compiler_params=_params(("parallel", "parallel")),
        name="pool",
    )(proj, proj, pool_w_bf, pool_scale.reshape(1, G * Cg))


SOLVE_BASE_LOG2 = 3


def _bdot(a, b):
    return jnp.dot(a.astype(BF16), b.astype(BF16), preferred_element_type=F32)


def _solve_unit_lower(L, X, ri, ci):
    b = SOLVE_BASE_LOG2
    A = jnp.where((ri >> b) == (ci >> b), L, 0.0)
    A2 = _bdot(A, A)
    yield
    A3 = _bdot(A, A2)
    A4 = _bdot(A2, A2)
    yield
    T0 = A2 - A - A3
    Tm = T0 + A4 + _bdot(T0, A4)
    yield
    lvl = b
    while (1 << lvl) < CHUNK:
        Cm = jnp.where(((ri >> (lvl + 1)) == (ci >> (lvl + 1))) & ((ri >> lvl) != (ci >> lvl)), L, 0.0)
        M = Cm + _bdot(Cm, Tm)
        yield
        Tm = Tm - M - _bdot(Tm, M)
        yield
        lvl += 1
    return X + _bdot(Tm, X)


def _round_robin(gens):
    out = [None] * len(gens)
    live = list(range(len(gens)))
    while live:
        for j in list(live):
            try:
                next(gens[j])
            except StopIteration as e:
                out[j] = e.value
                live.remove(j)
    return out


def _delta_kernel(q_ref, k_ref, v_ref, z_ref, ab_ref, wq_ref, wk_ref, wv_ref, nega_ref, dtb_ref,
                  hng_ref, o_ref, s_scr, halo_scr, qs_scr, ks_scr, vs_scr, g_scr, gt_scr, beta_scr,
                  *, hb, rows, n_heads):
    hg = pl.program_id(1)
    cb = pl.program_id(2)
    C = hb * HEAD_DIM

    @pl.when(cb == 0)
    def _():
        s_scr[...] = jnp.zeros_like(s_scr)
        halo_scr[...] = jnp.zeros_like(halo_scr)

    def conv(idx, x_ref, w_ref):
        x = x_ref[...].astype(F32)
        ext = jnp.concatenate([halo_scr[idx], x], axis=0)
        w = w_ref[...]
        y = x * w[CONV_WIDTH - 1:CONV_WIDTH]
        for s in range(1, CONV_WIDTH):
            y = y + ext[8 - s:8 - s + rows] * w[CONV_WIDTH - 1 - s:CONV_WIDTH - s]
        halo_scr[idx] = x[rows - 8:rows]
        return _silu(y)

    def l2n(t, mult):
        outs = []
        for j in range(hb):
            th = t[:, j * HEAD_DIM:(j + 1) * HEAD_DIM]
            outs.append(th * (lax.rsqrt(jnp.sum(th * th, axis=-1, keepdims=True) + EPS) * mult))
        return jnp.concatenate(outs, axis=1) if hb > 1 else outs[0]

    qs_scr[...] = l2n(conv(0, q_ref, wq_ref), HEAD_DIM ** -0.5)
    ks_scr[...] = l2n(conv(1, k_ref, wk_ref), 1.0)
    vs_scr[...] = conv(2, v_ref, wv_ref)

    ri = lax.broadcasted_iota(jnp.int32, (CHUNK, CHUNK), 0)
    ci = lax.broadcasted_iota(jnp.int32, (CHUNK, CHUNK), 1)
    causal = ri >= ci
    strict = ri > ci
    tri = causal.astype(F32)

    ab = ab_ref[...]
    ld = nega_ref[...] * jax.nn.softplus(ab + dtb_ref[...])
    beta_scr[...] = jax.nn.sigmoid(ab)
    for cc in range(rows // CHUNK):
        g = jnp.dot(tri, ld[cc * CHUNK:(cc + 1) * CHUNK], preferred_element_type=F32, precision=HIGHEST)
        g_scr[cc * CHUNK:(cc + 1) * CHUNK, :] = g
        gt_scr[cc] = g.T

    lane = lax.broadcasted_iota(jnp.int32, (CHUNK, LANES), 1)

    hng = hng_ref[...]

    def chunk_head(h, q, k, v, z, S, gch, g_row, btc):
        gb = jnp.broadcast_to(jnp.sum(jnp.where(lane == h, gch, 0.0), axis=-1, keepdims=True),
                              (CHUNK, LANES))
        beta_b = jnp.sum(jnp.where(lane == n_heads + h, btc, 0.0), axis=-1, keepdims=True)
        diff = gb[:, :CHUNK] - g_row
        decay = jnp.where(causal, jnp.exp(jnp.where(causal, diff, 0.0)), 0.0)
        eg = jnp.exp(gb)
        g_last = gb[CHUNK - 1:CHUNK, :]
        ew = jnp.exp(g_last - gb)
        e_last = jnp.exp(g_last)

        kb = k * beta_b
        kq = lax.dot_general(jnp.concatenate([kb, q], axis=0).astype(BF16), k.astype(BF16),
                             (((1,), (1,)), ((), ())), preferred_element_type=F32)
        yield
        L = jnp.where(strict, kq[:CHUNK] * decay, 0.0)
        intra = kq[CHUNK:] * decay
        X = jnp.concatenate([v * beta_b, kb * eg], axis=1)
        X = yield from _solve_unit_lower(L, X, ri, ci)
        yield
        Xb = X.astype(BF16)
        iv = jnp.dot(intra.astype(BF16), Xb, preferred_element_type=F32)
        kv = lax.dot_general((k * ew).astype(BF16), Xb, (((0,), (0,)), ((), ())),
                             preferred_element_type=F32)
        yield
        qe = q * eg - iv[:, HEAD_DIM:]
        lhs = jnp.concatenate([kv[:, HEAD_DIM:], qe], axis=0).astype(BF16)
        rs = jnp.dot(lhs, S.astype(BF16), preferred_element_type=F32)
        yield
        s_new = e_last * S - rs[:HEAD_DIM] + kv[:, :HEAD_DIM]
        o = rs[HEAD_DIM:] + iv[:, :HEAD_DIM]
        o = o * lax.rsqrt(jnp.mean(o * o, axis=-1, keepdims=True) + EPS) * hng
        return s_new, (o * _silu(z)).astype(o_ref.dtype)

    def chunk_body(c, carry):
        rsl = pl.ds(pl.multiple_of(c * CHUNK, CHUNK), CHUNK)
        gch = g_scr[rsl, :]
        btc = beta_scr[rsl, :]
        ins = []
        for j in range(hb):
            sl = slice(j * HEAD_DIM, (j + 1) * HEAD_DIM)
            ins.append((qs_scr[rsl, sl], ks_scr[rsl, sl], vs_scr[rsl, sl],
                        z_ref[rsl, sl].astype(F32), s_scr[j]))
        outs = _round_robin([chunk_head(hg * hb + j, *ins[j], gch, gt_scr[c, pl.ds(hg * hb + j, 1), :], btc)
                             for j in range(hb)])
        for j in range(hb):
            s_scr[j] = outs[j][0]
            o_ref[rsl, j * HEAD_DIM:(j + 1) * HEAD_DIM] = outs[j][1]
        return carry

    lax.fori_loop(0, rows // CHUNK, chunk_body, 0)


def _delta(proj, ab, conv_w, a_log, dt_bias, head_norm_g, *, batch, seq, n_heads, col0):
    T = proj.shape[0]
    KW = n_heads * HEAD_DIM
    hb = next(d for d in (16, 8, 4, 2, 1) if n_heads % d == 0)
    C = hb * HEAD_DIM
    rows = _tile(seq, 256)
    nb = seq // rows
    assert col0 % C == 0 and KW % C == 0 and n_heads <= LANES // 2
    qb, kb_, vb, zb = ((col0 + s * KW) // C for s in range(4))
    row = lambda b, g, c: b * nb + c
    pad = lambda t: jnp.zeros((1, LANES), F32).at[0, :n_heads].set(t)
    nega = pad(-jnp.exp(a_log.astype(F32)))
    dtb = pad(dt_bias.astype(F32))
    wq, wk, wv = (conv_w[:, s * KW:(s + 1) * KW] for s in range(3))
    kern = functools.partial(_delta_kernel, hb=hb, rows=rows, n_heads=n_heads)
    wspec = pl.BlockSpec((CONV_WIDTH, C), lambda b, g, c: (0, g))
    vec = pl.BlockSpec((1, LANES), lambda b, g, c: (0, 0))
    return pl.pallas_call(
        kern,
        grid=(batch, n_heads // hb, nb),
        in_specs=[pl.BlockSpec((rows, C), lambda b, g, c: (row(b, g, c), qb + g)),
                  pl.BlockSpec((rows, C), lambda b, g, c: (row(b, g, c), kb_ + g)),
                  pl.BlockSpec((rows, C), lambda b, g, c: (row(b, g, c), vb + g)),
                  pl.BlockSpec((rows, C), lambda b, g, c: (row(b, g, c), zb + g)),
                  pl.BlockSpec((rows, LANES), lambda b, g, c: (row(b, g, c), 0)),
                  wspec, wspec, wspec, vec, vec, vec],
        out_specs=pl.BlockSpec((rows, C), lambda b, g, c: (row(b, g, c), g)),
        out_shape=jax.ShapeDtypeStruct((T, KW), BF16),
        scratch_shapes=[pltpu.VMEM((hb, HEAD_DIM, HEAD_DIM), F32),
                        pltpu.VMEM((3, 8, C), F32),
                        pltpu.VMEM((rows, C), F32),
                        pltpu.VMEM((rows, C), F32),
                        pltpu.VMEM((rows, C), F32),
                        pltpu.VMEM((rows, LANES), F32),
                        pltpu.VMEM((rows // CHUNK, LANES, CHUNK), F32),
                        pltpu.VMEM((rows, LANES), F32)],
        compiler_params=_params(("parallel", "parallel", "arbitrary")),
        name="delta",
    )(proj, proj, proj, proj, ab, wq, wk, wv, nega, dtb, head_norm_g.reshape(1, HEAD_DIM).astype(F32))


def _merge_kernel(a1_ref, a2_ref, w1_ref, w2_ref, g1_ref, g2_ref, src_ref, o_ref, dst_ref):
    y1 = jnp.dot(a1_ref[...], w1_ref[...], preferred_element_type=F32)
    y2 = jnp.dot(a2_ref[...], w2_ref[...], preferred_element_type=F32)
    m = jax.nn.sigmoid(g1_ref[...].astype(F32)) * y1 + jax.nn.sigmoid(g2_ref[...].astype(F32)) * y2
    o_ref[...] = m.astype(o_ref.dtype)
    dst_ref[...] = src_ref[...].astype(dst_ref.dtype)


def _merge(yp, yd, w1, w2, gates, cast_along):
    T, K1 = yp.shape
    K2 = yd.shape[1]
    D = w1.shape[1]
    tm, tn = _tile(T, 1024), _tile(D, 512)
    ni, nj = T // tm, D // tn
    rows = cast_along.shape[0] // (ni * nj)
    assert rows * ni * nj == cast_along.shape[0] and rows % 16 == 0
    side = pl.BlockSpec((rows, cast_along.shape[1]), lambda i, j: (i * nj + j, 0))
    return pl.pallas_call(
        _merge_kernel,
        grid=(ni, nj),
        in_specs=[pl.BlockSpec((tm, K1), lambda i, j: (i, 0)),
                  pl.BlockSpec((tm, K2), lambda i, j: (i, 0)),
                  pl.BlockSpec((K1, tn), lambda i, j: (0, j)),
                  pl.BlockSpec((K2, tn), lambda i, j: (0, j)),
                  pl.BlockSpec((tm, tn), lambda i, j: (i, j)),
                  pl.BlockSpec((tm, tn), lambda i, j: (i, nj + j)),
                  side],
        out_specs=[pl.BlockSpec((tm, tn), lambda i, j: (i, j)), side],
        out_shape=[jax.ShapeDtypeStruct((T, D), BF16), jax.ShapeDtypeStruct(cast_along.shape, BF16)],
        compiler_params=_params(("arbitrary", "arbitrary")),
        name="merge",
    )(yp, yd, w1, w2, gates, gates, cast_along)


def _outproj_kernel(a_ref, w_ref, x_ref, gate_ref, o_ref):
    acc = jnp.dot(a_ref[...], w_ref[...], preferred_element_type=F32)
    o_ref[...] = x_ref[...] + gate_ref[0] * acc


def _outproj(a, w, x2, gate, seq):
    T, K = a.shape
    D = w.shape[1]
    tm, tn = _tile(seq, 1024), _tile(D, 1024)
    return pl.pallas_call(
        _outproj_kernel,
        grid=(T // tm, D // tn),
        in_specs=[pl.BlockSpec((tm, K), lambda i, j: (i, 0)),
                  pl.BlockSpec((K, tn), lambda i, j: (0, j)),
                  pl.BlockSpec((tm, tn), lambda i, j: (i, j)),
                  pl.BlockSpec((1, 1, tn), lambda i, j: ((i * tm) // seq, 0, j))],
        out_specs=pl.BlockSpec((tm, tn), lambda i, j: (i, j)),
        out_shape=jax.ShapeDtypeStruct((T, D), F32),
        compiler_params=_params(("parallel", "parallel")),
        name="outproj",
    )(a, w, x2, gate)


def _ffn_kernel(u_ref, w1_ref, w2_ref, h_hbm, gate_ref, fg_ref, o_ref, hbuf, hsem, *, nf, tn):
    i = pl.program_id(0)
    f = pl.program_id(1)
    tm = o_ref.shape[0]
    rows = hbuf.shape[1]
    nchunk = tm // rows

    def h_copy(r, slot):
        return pltpu.make_async_copy(h_hbm.at[pl.ds(i * tm + r * rows, rows), :], hbuf.at[slot], hsem.at[slot])

    @pl.when(f == 0)
    def _():
        o_ref[...] = jnp.zeros_like(o_ref)

    @pl.when(f == nf - 1)
    def _():
        h_copy(0, 0).start()

    hdn = jnp.maximum(jnp.dot(u_ref[...], w1_ref[...], preferred_element_type=F32), 0.0)
    hb = (hdn * hdn).astype(BF16)
    for n0 in range(0, o_ref.shape[1], tn):
        o_ref[:, n0:n0 + tn] += jnp.dot(hb, w2_ref[:, n0:n0 + tn], preferred_element_type=F32)

    @pl.when(f == nf - 1)
    def _():
        def body(r, carry):
            slot = lax.rem(r, 2)

            @pl.when(r + 1 < nchunk)
            def _():
                h_copy(r + 1, 1 - slot).start()

            h_copy(r, slot).wait()
            sl = pl.ds(pl.multiple_of(r * rows, rows), rows)
            h2 = hbuf[slot] + gate_ref[0] * o_ref[sl, :]
            o_ref[sl, :] = h2 * lax.rsqrt(jnp.mean(h2 * h2, axis=-1, keepdims=True) + EPS) * fg_ref[...]
            return carry

        lax.fori_loop(0, nchunk, body, 0)


def _ffn(u2, w1, w2, h, gate, final_g, seq):
    T, D = u2.shape
    F = w1.shape[1]
    tm, tf = _tile(seq, FFN_TM), _tile(F, FFN_TF)
    nf = F // tf
    once = dict(pipeline_mode=pl.Buffered(1))
    return pl.pallas_call(
        functools.partial(_ffn_kernel, nf=nf, tn=_tile(D, 1024)),
        grid=(T // tm, nf),
        in_specs=[pl.BlockSpec((tm, D), lambda i, f: (i, 0), **once),
                  pl.BlockSpec((D, tf), lambda i, f: (0, f)),
                  pl.BlockSpec((tf, D), lambda i, f: (f, 0)),
                  pl.BlockSpec(memory_space=pl.ANY),
                  pl.BlockSpec((1, 1, D), lambda i, f: ((i * tm) // seq, 0, 0)),
                  pl.BlockSpec((1, D), lambda i, f: (0, 0))],
        out_specs=pl.BlockSpec((tm, D), lambda i, f: (i, 0), **once),
        out_shape=jax.ShapeDtypeStruct((T, D), F32),
        scratch_shapes=[pltpu.VMEM((2, _tile(tm, FFN_EPILOGUE_ROWS), D), F32),
                        pltpu.SemaphoreType.DMA((2,))],
        compiler_params=_params(("arbitrary", "arbitrary")),
        name="ffn",
    )(u2, w1, w2, h, gate, final_g.reshape(1, D))


def kernel(x, c, w_ada, b_ada, norm1_g, w_in, conv_w, pool_w, pool_scale, a_log, dt_bias,
           head_norm_g, w_up_pool, w_up_delta, w_out, norm2_g, w_ff_in, w_ff_out, final_g):
    B, S, D = x.shape
    T = B * S
    H = a_log.shape[0]
    KW = H * HEAD_DIM
    PW = pool_scale.shape[0]
    assert w_in.shape[1] == PW + 3 * KW + KW + 2 * H + 2 * D
    assert w_ada.shape[1] == N_MOD * D and S % CHUNK == 0 and 2 * H <= LANES

    mod = _ada(c, w_ada, b_ada)
    shift1, scale1, gate1, shift2, scale2, gate2 = (
        mod[:, i * D:(i + 1) * D].reshape(B, 1, D) for i in range(N_MOD))

    c_ab = PW + 4 * KW
    w_bf = w_in.astype(BF16)
    w_gates = w_bf[:, c_ab + 2 * H:]
    w_ab = jnp.pad(w_bf[:, c_ab:c_ab + 2 * H], ((0, 0), (0, LANES - 2 * H)))

    x2 = x.reshape(T, D)
    u = _norm_mod(x2, norm1_g, shift1, scale1, S)
    proj, ab, _ = _matmul(u, w_bf, BF16, n_cols=c_ab, name="inproj", narrow=w_ab)
    G, Cg, _ = pool_w.shape
    gates, _, (w_up_pool, w_up_delta, w_out, w_ff_in, pool_w) = _matmul(
        u, w_gates, BF16, name="inproj_gates",
        cast_along=(w_up_pool, w_up_delta, w_out, w_ff_in, pool_w.reshape(G * Cg, Cg)))

    yp = _pool(proj, pool_w.reshape(G, Cg, Cg), pool_scale, S)
    yd = _delta(proj, ab, conv_w, a_log, dt_bias, head_norm_g,
                batch=B, seq=S, n_heads=H, col0=PW)
    merged, w_ff_out = _merge(yp, yd, w_up_pool, w_up_delta, gates, cast_along=w_ff_out)
    h = _outproj(merged, w_out, x2, gate1, S)

    u2 = _norm_mod(h, norm2_g, shift2, scale2, S)
    out = _ffn(u2, w_ff_in, w_ff_out, h, gate2, final_g, S)
    return out.reshape(B, S, D)
```

```python
import functools

import jax
import jax.numpy as jnp
from jax import lax
from jax.experimental import pallas as pl
from jax.experimental.pallas import tpu as pltpu

EPS = 1e-6
HEAD_DIM = 128
CHUNK = 64
CONV_WIDTH = 4
POOL_WINDOWS = (2, 4, 8, 16)
N_MOD = 6
LANES = 128
VMEM_LIMIT = 56 * 1024 * 1024
FFN_TM, FFN_TF = 1024, 512
FFN_EPILOGUE_ROWS = 128
FFN_VMEM_LIMIT = 60 * 1024 * 1024
F32 = jnp.float32
BF16 = jnp.bfloat16
HIGHEST = lax.Precision.HIGHEST


def _params(sem):
    return pltpu.CompilerParams(dimension_semantics=sem, vmem_limit_bytes=VMEM_LIMIT)


def _tile(n, pref):
    t = min(n, pref)
    while n % t:
        t //= 2
    return t


def _silu(x):
    return x * jax.nn.sigmoid(x)


def _ada_kernel(c_ref, w_ref, b_ref, o_ref):
    s = _silu(c_ref[...]).astype(BF16)
    o_ref[...] = jnp.dot(s, w_ref[...].astype(BF16), preferred_element_type=F32) + b_ref[...]


def _ada(c, w_ada, b_ada):
    B, D = c.shape
    N = w_ada.shape[1]
    rows = 8
    cp = jnp.zeros((rows, D), F32).at[:B].set(c)
    tn = _tile(N, 512)
    out = pl.pallas_call(
        _ada_kernel,
        grid=(N // tn,),
        in_specs=[pl.BlockSpec((rows, D), lambda j: (0, 0)),
                  pl.BlockSpec((D, tn), lambda j: (0, j)),
                  pl.BlockSpec((1, tn), lambda j: (0, j))],
        out_specs=pl.BlockSpec((rows, tn), lambda j: (0, j)),
        out_shape=jax.ShapeDtypeStruct((rows, N), F32),
        compiler_params=_params(("parallel",)),
        name="ada",
    )(cp, w_ada, b_ada.reshape(1, N))
    return out[:B]


def _norm_kernel(x_ref, g_ref, shift_ref, scale_ref, o_ref):
    x = x_ref[...]
    n = x * lax.rsqrt(jnp.mean(x * x, axis=-1, keepdims=True) + EPS) * g_ref[...]
    o_ref[...] = (n * (1.0 + scale_ref[0]) + shift_ref[0]).astype(o_ref.dtype)


def _norm_mod(x2, g, shift, scale, seq):
    T, D = x2.shape
    tm = _tile(seq, 512)
    bmap = lambda i: ((i * tm) // seq, 0, 0)
    return pl.pallas_call(
        _norm_kernel,
        grid=(T // tm,),
        in_specs=[pl.BlockSpec((tm, D), lambda i: (i, 0)),
                  pl.BlockSpec((1, D), lambda i: (0, 0)),
                  pl.BlockSpec((1, 1, D), bmap),
                  pl.BlockSpec((1, 1, D), bmap)],
        out_specs=pl.BlockSpec((tm, D), lambda i: (i, 0)),
        out_shape=jax.ShapeDtypeStruct((T, D), BF16),
        compiler_params=_params(("parallel",)),
        name="norm_mod",
    )(x2, g.reshape(1, D), shift, scale)


def _mm_kernel(a_ref, w_ref, *refs, n_narrow):
    n_side = (len(refs) - 1 - 2 * n_narrow) // 2
    ins, outs = refs[:n_narrow + n_side], refs[n_narrow + n_side:]
    o_ref = outs[0]
    o_ref[...] = jnp.dot(a_ref[...], w_ref[...], preferred_element_type=F32).astype(o_ref.dtype)
    if n_narrow:
        @pl.when(pl.program_id(1) == 0)
        def _():
            outs[1][...] = jnp.dot(a_ref[...], ins[0][...], preferred_element_type=F32)
    for src, dst in zip(ins[n_narrow:], outs[1 + n_narrow:]):
        dst[...] = src[...].astype(dst.dtype)


def _matmul(a, w, out_dtype, n_cols=None, tm_pref=1024, tn_pref=1024, name="mm", narrow=None, cast_along=()):
    M, K = a.shape
    N = w.shape[1] if n_cols is None else n_cols
    tm, tn = _tile(M, tm_pref), _tile(N, tn_pref)
    ni, nj = M // tm, N // tn
    steps = ni * nj
    side_specs = []
    for t in cast_along:
        rows = t.shape[0] // steps
        assert rows * steps == t.shape[0] and rows % 16 == 0, (t.shape, steps)
        side_specs.append(pl.BlockSpec((rows, t.shape[1]), lambda i, j: (i * nj + j, 0)))
    nar_in = [] if narrow is None else [pl.BlockSpec(narrow.shape, lambda i, j: (0, 0))]
    nar_out = [] if narrow is None else [pl.BlockSpec((tm, narrow.shape[1]), lambda i, j: (i, 0))]
    nar_shape = [] if narrow is None else [jax.ShapeDtypeStruct((M, narrow.shape[1]), F32)]
    outs = pl.pallas_call(
        functools.partial(_mm_kernel, n_narrow=len(nar_in)),
        grid=(ni, nj),
        in_specs=[pl.BlockSpec((tm, K), lambda i, j: (i, 0)),
                  pl.BlockSpec((K, tn), lambda i, j: (0, j))] + nar_in + side_specs,
        out_specs=[pl.BlockSpec((tm, tn), lambda i, j: (i, j))] + nar_out + side_specs,
        out_shape=[jax.ShapeDtypeStruct((M, N), out_dtype)] + nar_shape
                  + [jax.ShapeDtypeStruct(t.shape, BF16) for t in cast_along],
        compiler_params=_params(("arbitrary", "arbitrary")),
        name=name,
    )(a, w, *([] if narrow is None else [narrow]), *cast_along)
    return outs[0], (None if narrow is None else outs[1]), outs[1 + len(nar_in):]


def _pool_kernel(p_ref, halo_ref, w_ref, scale_ref, o_ref, *, seq, tm, halo):
    i = pl.program_id(0)
    g = pl.program_id(1)
    start = (i * tm) % seq
    p = p_ref[...]
    hl = jnp.where(start > 0, halo_ref[...], jnp.zeros_like(halo_ref))
    ext = jnp.concatenate([jnp.zeros((LANES - halo, p.shape[1]), p.dtype), hl, p], axis=0)
    win = jnp.where(g == 0, POOL_WINDOWS[0],
                    jnp.where(g == 1, POOL_WINDOWS[1],
                              jnp.where(g == 2, POOL_WINDOWS[2], POOL_WINDOWS[3])))
    r = lax.broadcasted_iota(jnp.int32, (tm, tm + LANES), 0)
    cidx = lax.broadcasted_iota(jnp.int32, (tm, tm + LANES), 1)
    d = r + LANES - cidx
    band = jnp.where((d >= 0) & (d < win), 1.0, 0.0).astype(BF16)
    wsum = jnp.dot(band, ext, preferred_element_type=F32)
    pos = start + lax.broadcasted_iota(jnp.int32, (tm, 1), 0) + 1
    count = jnp.minimum(pos, win).astype(F32)
    mixed = wsum / count - p.astype(F32)
    y = jnp.dot(mixed.astype(BF16), w_ref[0], preferred_element_type=F32)
    o_ref[...] = (y * scale_ref[...]).astype(o_ref.dtype)


def _pool(proj, pool_w_bf, pool_scale, seq):
    T = proj.shape[0]
    G, Cg, _ = pool_w_bf.shape
    assert G == len(POOL_WINDOWS)
    halo = 16
    tm = _tile(seq, 512)
    hb = tm // halo
    kern = functools.partial(_pool_kernel, seq=seq, tm=tm, halo=halo)
    return pl.pallas_call(
        kern,
        grid=(T // tm, G),
        in_specs=[pl.BlockSpec((tm, Cg), lambda i, g: (i, g)),
                  pl.BlockSpec((halo, Cg), lambda i, g: (jnp.maximum(i * hb - 1, 0), g)),
                  pl.BlockSpec((1, Cg, Cg), lambda i, g: (g, 0, 0)),
                  pl.BlockSpec((1, Cg), lambda i, g: (0, g))],
        out_specs=pl.BlockSpec((tm, Cg), lambda i, g: (i, g)),
        out_shape=jax.ShapeDtypeStruct((T, G * Cg), BF16),
        compiler_params=_params(("parallel", "parallel")),
        name="pool",
    )(proj, proj, pool_w_bf, pool_scale.reshape(1, G * Cg))


SOLVE_BASE_LOG2 = 3


def _bdot(a, b):
    return jnp.dot(a.astype(BF16), b.astype(BF16), preferred_element_type=F32)


def _solve_unit_lower(L, X, ri, ci):
    b = SOLVE_BASE_LOG2
    A = jnp.where((ri >> b) == (ci >> b), L, 0.0)
    A2 = _bdot(A, A)
    yield
    A3 = _bdot(A, A2)
    A4 = _bdot(A2, A2)
    yield
    T0 = A2 - A - A3
    Tm = T0 + A4 + _bdot(T0, A4)
    yield
    lvl = b
    while (1 << lvl) < CHUNK:
        Cm = jnp.where(((ri >> (lvl + 1)) == (ci >> (lvl + 1))) & ((ri >> lvl) != (ci >> lvl)), L, 0.0)
        M = Cm + _bdot(Cm, Tm)
        yield
        Tm = Tm - M - _bdot(Tm, M)
        yield
        lvl += 1
    return X + _bdot(Tm, X)


def _round_robin(gens):
    out = [None] * len(gens)
    live = list(range(len(gens)))
    while live:
        for j in list(live):
            try:
                next(gens[j])
            except StopIteration as e:
                out[j] = e.value
                live.remove(j)
    return out


def _delta_kernel(q_ref, k_ref, v_ref, z_ref, ab_ref, wq_ref, wk_ref, wv_ref, nega_ref, dtb_ref,
                  hng_ref, o_ref, s_scr, halo_scr, qs_scr, ks_scr, vs_scr, g_scr, gt_scr, beta_scr,
                  *, hb, rows, n_heads):
    hg = pl.program_id(1)
    cb = pl.program_id(2)

    @pl.when(cb == 0)
    def _():
        s_scr[...] = jnp.zeros_like(s_scr)
        halo_scr[...] = jnp.zeros_like(halo_scr)

    def conv(idx, x_ref, w_ref):
        x = x_ref[...].astype(F32)
        ext = jnp.concatenate([halo_scr[idx], x], axis=0)
        w = w_ref[...]
        y = x * w[CONV_WIDTH - 1:CONV_WIDTH]
        for s in range(1, CONV_WIDTH):
            y = y + ext[8 - s:8 - s + rows] * w[CONV_WIDTH - 1 - s:CONV_WIDTH - s]
        halo_scr[idx] = x[rows - 8:rows]
        return _silu(y)

    def l2n(t, mult):
        outs = []
        for j in range(hb):
            th = t[:, j * HEAD_DIM:(j + 1) * HEAD_DIM]
            outs.append(th * (lax.rsqrt(jnp.sum(th * th, axis=-1, keepdims=True) + EPS) * mult))
        return jnp.concatenate(outs, axis=1) if hb > 1 else outs[0]

    qs_scr[...] = l2n(conv(0, q_ref, wq_ref), HEAD_DIM ** -0.5)
    ks_scr[...] = l2n(conv(1, k_ref, wk_ref), 1.0)
    vs_scr[...] = conv(2, v_ref, wv_ref)

    ri = lax.broadcasted_iota(jnp.int32, (CHUNK, CHUNK), 0)
    ci = lax.broadcasted_iota(jnp.int32, (CHUNK, CHUNK), 1)
    causal = ri >= ci
    strict = ri > ci
    tri = causal.astype(F32)

    ab = ab_ref[...]
    ld = nega_ref[...] * jax.nn.softplus(ab + dtb_ref[...])
    beta_scr[...] = jax.nn.sigmoid(ab)
    for cc in range(rows // CHUNK):
        g = jnp.dot(tri, ld[cc * CHUNK:(cc + 1) * CHUNK], preferred_element_type=F32, precision=HIGHEST)
        g_scr[cc * CHUNK:(cc + 1) * CHUNK, :] = g
        gt_scr[cc] = g.T

    lane = lax.broadcasted_iota(jnp.int32, (CHUNK, LANES), 1)

    hng = hng_ref[...]

    def chunk_head(h, q, k, v, z, S, gch, g_row, btc):
        gb = jnp.broadcast_to(jnp.sum(jnp.where(lane == h, gch, 0.0), axis=-1, keepdims=True),
                              (CHUNK, LANES))
        beta_b = jnp.sum(jnp.where(lane == n_heads + h, btc, 0.0), axis=-1, keepdims=True)
        diff = gb[:, :CHUNK] - g_row
        decay = jnp.where(causal, jnp.exp(jnp.where(causal, diff, 0.0)), 0.0)
        eg = jnp.exp(gb)
        g_last = gb[CHUNK - 1:CHUNK, :]
        ew = jnp.exp(g_last - gb)
        e_last = jnp.exp(g_last)

        kb = k * beta_b
        kq = lax.dot_general(jnp.concatenate([kb, q], axis=0).astype(BF16), k.astype(BF16),
                             (((1,), (1,)), ((), ())), preferred_element_type=F32)
        yield
        L = jnp.where(strict, kq[:CHUNK] * decay, 0.0)
        intra = kq[CHUNK:] * decay
        X = jnp.concatenate([v * beta_b, kb * eg], axis=1)
        X = yield from _solve_unit_lower(L, X, ri, ci)
        yield
        Xb = X.astype(BF16)
        iv = jnp.dot(intra.astype(BF16), Xb, preferred_element_type=F32)
        kv = lax.dot_general((k * ew).astype(BF16), Xb, (((0,), (0,)), ((), ())),
                             preferred_element_type=F32)
        yield
        qe = q * eg - iv[:, HEAD_DIM:]
        lhs = jnp.concatenate([kv[:, HEAD_DIM:], qe], axis=0).astype(BF16)
        rs = jnp.dot(lhs, S.astype(BF16), preferred_element_type=F32)
        yield
        s_new = e_last * S - rs[:HEAD_DIM] + kv[:, :HEAD_DIM]
        o = rs[HEAD_DIM:] + iv[:, :HEAD_DIM]
        o = o * lax.rsqrt(jnp.mean(o * o, axis=-1, keepdims=True) + EPS) * hng
        return s_new, (o * _silu(z)).astype(o_ref.dtype)

    def chunk_body(c, carry):
        rsl = pl.ds(pl.multiple_of(c * CHUNK, CHUNK), CHUNK)
        gch = g_scr[rsl, :]
        btc = beta_scr[rsl, :]
        ins = []
        for j in range(hb):
            sl = slice(j * HEAD_DIM, (j + 1) * HEAD_DIM)
            ins.append((qs_scr[rsl, sl], ks_scr[rsl, sl], vs_scr[rsl, sl],
                        z_ref[rsl, sl].astype(F32), s_scr[j]))
        outs = _round_robin([chunk_head(hg * hb + j, *ins[j], gch, gt_scr[c, pl.ds(hg * hb + j, 1), :], btc)
                             for j in range(hb)])
        for j in range(hb):
            s_scr[j] = outs[j][0]
            o_ref[rsl, j * HEAD_DIM:(j + 1) * HEAD_DIM] = outs[j][1]
        return carry

    lax.fori_loop(0, rows // CHUNK, chunk_body, 0)


def _delta(proj, ab, conv_w, a_log, dt_bias, head_norm_g, *, batch, seq, n_heads, col0):
    T = proj.shape[0]
    KW = n_heads * HEAD_DIM
    hb = next(d for d in (16, 8, 4, 2, 1) if n_heads % d == 0)
    C = hb * HEAD_DIM
    rows = _tile(seq, 256)
    nb = seq // rows
    assert col0 % C == 0 and KW % C == 0 and n_heads <= LANES // 2
    qb, kb_, vb, zb = ((col0 + s * KW) // C for s in range(4))
    row = lambda b, g, c: b * nb + c
    pad = lambda t: jnp.zeros((1, LANES), F32).at[0, :n_heads].set(t)
    nega = pad(-jnp.exp(a_log.astype(F32)))
    dtb = pad(dt_bias.astype(F32))
    wq, wk, wv = (conv_w[:, s * KW:(s + 1) * KW] for s in range(3))
    kern = functools.partial(_delta_kernel, hb=hb, rows=rows, n_heads=n_heads)
    wspec = pl.BlockSpec((CONV_WIDTH, C), lambda b, g, c: (0, g))
    vec = pl.BlockSpec((1, LANES), lambda b, g, c: (0, 0))
    return pl.pallas_call(
        kern,
        grid=(batch, n_heads // hb, nb),
        in_specs=[pl.BlockSpec((rows, C), lambda b, g, c: (row(b, g, c), qb + g)),
                  pl.BlockSpec((rows, C), lambda b, g, c: (row(b, g, c), kb_ + g)),
                  pl.BlockSpec((rows, C), lambda b, g, c: (row(b, g, c), vb + g)),
                  pl.BlockSpec((rows, C), lambda b, g, c: (row(b, g, c), zb + g)),
                  pl.BlockSpec((rows, LANES), lambda b, g, c: (row(b, g, c), 0)),
                  wspec, wspec, wspec, vec, vec, vec],
        out_specs=pl.BlockSpec((rows, C), lambda b, g, c: (row(b, g, c), g)),
        out_shape=jax.ShapeDtypeStruct((T, KW), BF16),
        scratch_shapes=[pltpu.VMEM((hb, HEAD_DIM, HEAD_DIM), F32),
                        pltpu.VMEM((3, 8, C), F32),
                        pltpu.VMEM((rows, C), F32),
                        pltpu.VMEM((rows, C), F32),
                        pltpu.VMEM((rows, C), F32),
                        pltpu.VMEM((rows, LANES), F32),
                        pltpu.VMEM((rows // CHUNK, LANES, CHUNK), F32),
                        pltpu.VMEM((rows, LANES), F32)],
        compiler_params=_params(("parallel", "parallel", "arbitrary")),
        name="delta",
    )(proj, proj, proj, proj, ab, wq, wk, wv, nega, dtb, head_norm_g.reshape(1, HEAD_DIM).astype(F32))


def _merge_kernel(a1_ref, a2_ref, w1_ref, w2_ref, g1_ref, g2_ref, src_ref, o_ref, dst_ref):
    y1 = jnp.dot(a1_ref[...], w1_ref[...], preferred_element_type=F32)
    y2 = jnp.dot(a2_ref[...], w2_ref[...], preferred_element_type=F32)
    m = jax.nn.sigmoid(g1_ref[...].astype(F32)) * y1 + jax.nn.sigmoid(g2_ref[...].astype(F32)) * y2
    o_ref[...] = m.astype(o_ref.dtype)
    dst_ref[...] = src_ref[...].astype(dst_ref.dtype)


def _merge(yp, yd, w1, w2, gates, cast_along):
    T, K1 = yp.shape
    K2 = yd.shape[1]
    D = w1.shape[1]
    tm, tn = _tile(T, 1024), _tile(D, 512)
    ni, nj = T // tm, D // tn
    rows = cast_along.shape[0] // (ni * nj)
    assert rows * ni * nj == cast_along.shape[0] and rows % 16 == 0
    side = pl.BlockSpec((rows, cast_along.shape[1]), lambda i, j: (i * nj + j, 0))
    return pl.pallas_call(
        _merge_kernel,
        grid=(ni, nj),
        in_specs=[pl.BlockSpec((tm, K1), lambda i, j: (i, 0)),
                  pl.BlockSpec((tm, K2), lambda i, j: (i, 0)),
                  pl.BlockSpec((K1, tn), lambda i, j: (0, j)),
                  pl.BlockSpec((K2, tn), lambda i, j: (0, j)),
                  pl.BlockSpec((tm, tn), lambda i, j: (i, j)),
                  pl.BlockSpec((tm, tn), lambda i, j: (i, nj + j)),
                  side],
        out_specs=[pl.BlockSpec((tm, tn), lambda i, j: (i, j)), side],
        out_shape=[jax.ShapeDtypeStruct((T, D), BF16), jax.ShapeDtypeStruct(cast_along.shape, BF16)],
        compiler_params=_params(("arbitrary", "arbitrary")),
        name="merge",
    )(yp, yd, w1, w2, gates, gates, cast_along)


def _outproj_kernel(a_ref, w_ref, x_ref, gate_ref, o_ref):
    acc = jnp.dot(a_ref[...], w_ref[...], preferred_element_type=F32)
    o_ref[...] = x_ref[...] + gate_ref[0] * acc


def _outproj(a, w, x2, gate, seq):
    T, K = a.shape
    D = w.shape[1]
    tm, tn = _tile(seq, 1024), _tile(D, 1024)
    return pl.pallas_call(
        _outproj_kernel,
        grid=(T // tm, D // tn),
        in_specs=[pl.BlockSpec((tm, K), lambda i, j: (i, 0)),
                  pl.BlockSpec((K, tn), lambda i, j: (0, j)),
                  pl.BlockSpec((tm, tn), lambda i, j: (i, j)),
                  pl.BlockSpec((1, 1, tn), lambda i, j: ((i * tm) // seq, 0, j))],
        out_specs=pl.BlockSpec((tm, tn), lambda i, j: (i, j)),
        out_shape=jax.ShapeDtypeStruct((T, D), F32),
        compiler_params=_params(("parallel", "parallel")),
        name="outproj",
    )(a, w, x2, gate)


def _ffn_kernel(u_ref, w1_ref, w2_ref, h_hbm, gate_ref, fg_ref, o_ref, hbuf, hsem, *, nf, tn):
    i = pl.program_id(0)
    f = pl.program_id(1)
    tm = o_ref.shape[0]
    rows = hbuf.shape[1]
    nchunk = tm // rows

    def h_copy(r, slot):
        return pltpu.make_async_copy(h_hbm.at[pl.ds(i * tm + r * rows, rows), :], hbuf.at[slot], hsem.at[slot])

    @pl.when(f == 0)
    def _():
        o_ref[...] = jnp.zeros_like(o_ref)

    @pl.when(f == nf - 1)
    def _():
        h_copy(0, 0).start()

    hdn = jnp.maximum(jnp.dot(u_ref[...], w1_ref[...], preferred_element_type=F32), 0.0)
    hb = (hdn * hdn).astype(BF16)
    for n0 in range(0, o_ref.shape[1], tn):
        o_ref[:, n0:n0 + tn] += jnp.dot(hb, w2_ref[:, n0:n0 + tn], preferred_element_type=F32)

    @pl.when(f == nf - 1)
    def _():
        def body(r, carry):
            slot = lax.rem(r, 2)

            @pl.when(r + 1 < nchunk)
            def _():
                h_copy(r + 1, 1 - slot).start()

            h_copy(r, slot).wait()
            sl = pl.ds(pl.multiple_of(r * rows, rows), rows)
            h2 = hbuf[slot] + gate_ref[0] * o_ref[sl, :]
            o_ref[sl, :] = h2 * lax.rsqrt(jnp.mean(h2 * h2, axis=-1, keepdims=True) + EPS) * fg_ref[...]
            return carry

        lax.fori_loop(0, nchunk, body, 0)


def _ffn(u2, w1, w2, h, gate, final_g, seq):
    T, D = u2.shape
    F = w1.shape[1]
    tm, tf = _tile(seq, FFN_TM), _tile(F, FFN_TF)
    nf = F // tf
    once = dict(pipeline_mode=pl.Buffered(1))
    return pl.pallas_call(
        functools.partial(_ffn_kernel, nf=nf, tn=_tile(D, 1024)),
        grid=(T // tm, nf),
        in_specs=[pl.BlockSpec((tm, D), lambda i, f: (i, 0)),
                  pl.BlockSpec((D, tf), lambda i, f: (0, f)),
                  pl.BlockSpec((tf, D), lambda i, f: (f, 0)),
                  pl.BlockSpec(memory_space=pl.ANY),
                  pl.BlockSpec((1, 1, D), lambda i, f: ((i * tm) // seq, 0, 0)),
                  pl.BlockSpec((1, D), lambda i, f: (0, 0))],
        out_specs=pl.BlockSpec((tm, D), lambda i, f: (i, 0), **once),
        out_shape=jax.ShapeDtypeStruct((T, D), F32),
        scratch_shapes=[pltpu.VMEM((2, _tile(tm, FFN_EPILOGUE_ROWS), D), F32),
                        pltpu.SemaphoreType.DMA((2,))],
        compiler_params=pltpu.CompilerParams(dimension_semantics=("arbitrary", "arbitrary"),
                                             vmem_limit_bytes=FFN_VMEM_LIMIT),
        name="ffn",
    )(u2, w1, w2, h, gate, final_g.reshape(1, D))


def kernel(x, c, w_ada, b_ada, norm1_g, w_in, conv_w, pool_w, pool_scale, a_log, dt_bias,
           head_norm_g, w_up_pool, w_up_delta, w_out, norm2_g, w_ff_in, w_ff_out, final_g):
    B, S, D = x.shape
    T = B * S
    H = a_log.shape[0]
    KW = H * HEAD_DIM
    PW = pool_scale.shape[0]
    assert w_in.shape[1] == PW + 3 * KW + KW + 2 * H + 2 * D
    assert w_ada.shape[1] == N_MOD * D and S % CHUNK == 0 and 2 * H <= LANES

    mod = _ada(c, w_ada, b_ada)
    shift1, scale1, gate1, shift2, scale2, gate2 = (
        mod[:, i * D:(i + 1) * D].reshape(B, 1, D) for i in range(N_MOD))

    c_ab = PW + 4 * KW
    w_bf = w_in.astype(BF16)
    w_gates = w_bf[:, c_ab + 2 * H:]
    w_ab = jnp.pad(w_bf[:, c_ab:c_ab + 2 * H], ((0, 0), (0, LANES - 2 * H)))

    x2 = x.reshape(T, D)
    u = _norm_mod(x2, norm1_g, shift1, scale1, S)
    proj, ab, _ = _matmul(u, w_bf, BF16, n_cols=c_ab, name="inproj", narrow=w_ab)
    G, Cg, _ = pool_w.shape
    gates, _, (w_up_pool, w_up_delta, w_out, w_ff_in, pool_w) = _matmul(
        u, w_gates, BF16, name="inproj_gates",
        cast_along=(w_up_pool, w_up_delta, w_out, w_ff_in, pool_w.reshape(G * Cg, Cg)))

    yp = _pool(proj, pool_w.reshape(G, Cg, Cg), pool_scale, S)
    yd = _delta(proj, ab, conv_w, a_log, dt_bias, head_norm_g,
                batch=B, seq=S, n_heads=H, col0=PW)
    merged, w_ff_out = _merge(yp, yd, w_up_pool, w_up_delta, gates, cast_along=w_ff_out)
    h = _outproj(merged, w_out, x2, gate1, S)

    u2 = _norm_mod(h, norm2_g, shift2, scale2, S)
    out = _ffn(u2, w_ff_in, w_ff_out, h, gate2, final_g, S)
    return out.reshape(B, S, D)
```

```python
import functools

import jax
import jax.numpy as jnp
from jax import lax
from jax.experimental import pallas as pl
from jax.experimental.pallas import tpu as pltpu

EPS = 1e-6
HEAD_DIM = 128
CHUNK = 64
CONV_WIDTH = 4
POOL_WINDOWS = (2, 4, 8, 16)
N_MOD = 6
LANES = 128
VMEM_LIMIT = 56 * 1024 * 1024
FFN_TM, FFN_TF = 1024, 512
FFN_EPILOGUE_ROWS = 128
FFN_VMEM_LIMIT = 60 * 1024 * 1024
F32 = jnp.float32
BF16 = jnp.bfloat16
HIGHEST = lax.Precision.HIGHEST


def _params(sem):
    return pltpu.CompilerParams(dimension_semantics=sem, vmem_limit_bytes=VMEM_LIMIT)


def _tile(n, pref):
    t = min(n, pref)
    while n % t:
        t //= 2
    return t


def _silu(x):
    return x * jax.nn.sigmoid(x)


def _ada_kernel(c_ref, w_ref, b_ref, o_ref):
    s = _silu(c_ref[...]).astype(BF16)
    o_ref[...] = jnp.dot(s, w_ref[...].astype(BF16), preferred_element_type=F32) + b_ref[...]


def _ada(c, w_ada, b_ada):
    B, D = c.shape
    N = w_ada.shape[1]
    rows = 8
    cp = jnp.zeros((rows, D), F32).at[:B].set(c)
    tn = _tile(N, 512)
    out = pl.pallas_call(
        _ada_kernel,
        grid=(N // tn,),
        in_specs=[pl.BlockSpec((rows, D), lambda j: (0, 0)),
                  pl.BlockSpec((D, tn), lambda j: (0, j)),
                  pl.BlockSpec((1, tn), lambda j: (0, j))],
        out_specs=pl.BlockSpec((rows, tn), lambda j: (0, j)),
        out_shape=jax.ShapeDtypeStruct((rows, N), F32),
        compiler_params=_params(("parallel",)),
        name="ada",
    )(cp, w_ada, b_ada.reshape(1, N))
    return out[:B]


def _norm_kernel(x_ref, g_ref, shift_ref, scale_ref, o_ref):
    x = x_ref[...]
    n = x * lax.rsqrt(jnp.mean(x * x, axis=-1, keepdims=True) + EPS) * g_ref[...]
    o_ref[...] = (n * (1.0 + scale_ref[0]) + shift_ref[0]).astype(o_ref.dtype)


def _norm_mod(x2, g, shift, scale, seq):
    T, D = x2.shape
    tm = _tile(seq, 512)
    bmap = lambda i: ((i * tm) // seq, 0, 0)
    return pl.pallas_call(
        _norm_kernel,
        grid=(T // tm,),
        in_specs=[pl.BlockSpec((tm, D), lambda i: (i, 0)),
                  pl.BlockSpec((1, D), lambda i: (0, 0)),
                  pl.BlockSpec((1, 1, D), bmap),
                  pl.BlockSpec((1, 1, D), bmap)],
        out_specs=pl.BlockSpec((tm, D), lambda i: (i, 0)),
        out_shape=jax.ShapeDtypeStruct((T, D), BF16),
        compiler_params=_params(("parallel",)),
        name="norm_mod",
    )(x2, g.reshape(1, D), shift, scale)


def _mm_kernel(a_ref, w_ref, *refs, n_narrow):
    n_side = (len(refs) - 1 - 2 * n_narrow) // 2
    ins, outs = refs[:n_narrow + n_side], refs[n_narrow + n_side:]
    o_ref = outs[0]
    o_ref[...] = jnp.dot(a_ref[...], w_ref[...], preferred_element_type=F32).astype(o_ref.dtype)
    if n_narrow:
        @pl.when(pl.program_id(1) == 0)
        def _():
            outs[1][...] = jnp.dot(a_ref[...], ins[0][...], preferred_element_type=F32)
    for src, dst in zip(ins[n_narrow:], outs[1 + n_narrow:]):
        dst[...] = src[...].astype(dst.dtype)


def _matmul(a, w, out_dtype, n_cols=None, tm_pref=1024, tn_pref=1024, name="mm", narrow=None, cast_along=()):
    M, K = a.shape
    N = w.shape[1] if n_cols is None else n_cols
    tm, tn = _tile(M, tm_pref), _tile(N, tn_pref)
    ni, nj = M // tm, N // tn
    steps = ni * nj
    side_specs = []
    for t in cast_along:
        rows = t.shape[0] // steps
        assert rows * steps == t.shape[0] and rows % 16 == 0, (t.shape, steps)
        side_specs.append(pl.BlockSpec((rows, t.shape[1]), lambda i, j: (i * nj + j, 0)))
    nar_in = [] if narrow is None else [pl.BlockSpec(narrow.shape, lambda i, j: (0, 0))]
    nar_out = [] if narrow is None else [pl.BlockSpec((tm, narrow.shape[1]), lambda i, j: (i, 0))]
    nar_shape = [] if narrow is None else [jax.ShapeDtypeStruct((M, narrow.shape[1]), F32)]
    outs = pl.pallas_call(
        functools.partial(_mm_kernel, n_narrow=len(nar_in)),
        grid=(ni, nj),
        in_specs=[pl.BlockSpec((tm, K), lambda i, j: (i, 0)),
                  pl.BlockSpec((K, tn), lambda i, j: (0, j))] + nar_in + side_specs,
        out_specs=[pl.BlockSpec((tm, tn), lambda i, j: (i, j))] + nar_out + side_specs,
        out_shape=[jax.ShapeDtypeStruct((M, N), out_dtype)] + nar_shape
                  + [jax.ShapeDtypeStruct(t.shape, BF16) for t in cast_along],
        compiler_params=_params(("arbitrary", "arbitrary")),
        name=name,
    )(a, w, *([] if narrow is None else [narrow]), *cast_along)
    return outs[0], (None if narrow is None else outs[1]), outs[1 + len(nar_in):]


def _pool_kernel(p_ref, halo_ref, w_ref, scale_ref, o_ref, *, seq, tm, halo):
    i = pl.program_id(0)
    g = pl.program_id(1)
    start = (i * tm) % seq
    p = p_ref[...]
    hl = jnp.where(start > 0, halo_ref[...], jnp.zeros_like(halo_ref))
    ext = jnp.concatenate([jnp.zeros((LANES - halo, p.shape[1]), p.dtype), hl, p], axis=0)
    win = jnp.where(g == 0, POOL_WINDOWS[0],
                    jnp.where(g == 1, POOL_WINDOWS[1],
                              jnp.where(g == 2, POOL_WINDOWS[2], POOL_WINDOWS[3])))
    r = lax.broadcasted_iota(jnp.int32, (tm, tm + LANES), 0)
    cidx = lax.broadcasted_iota(jnp.int32, (tm, tm + LANES), 1)
    d = r + LANES - cidx
    band = jnp.where((d >= 0) & (d < win), 1.0, 0.0).astype(BF16)
    wsum = jnp.dot(band, ext, preferred_element_type=F32)
    pos = start + lax.broadcasted_iota(jnp.int32, (tm, 1), 0) + 1
    count = jnp.minimum(pos, win).astype(F32)
    mixed = wsum / count - p.astype(F32)
    y = jnp.dot(mixed.astype(BF16), w_ref[0], preferred_element_type=F32)
    o_ref[...] = (y * scale_ref[...]).astype(o_ref.dtype)


def _pool(proj, pool_w_bf, pool_scale, seq):
    T = proj.shape[0]
    G, Cg, _ = pool_w_bf.shape
    assert G == len(POOL_WINDOWS)
    halo = 16
    tm = _tile(seq, 512)
    hb = tm // halo
    kern = functools.partial(_pool_kernel, seq=seq, tm=tm, halo=halo)
    return pl.pallas_call(
        kern,
        grid=(T // tm, G),
        in_specs=[pl.BlockSpec((tm, Cg), lambda i, g: (i, g)),
                  pl.BlockSpec((halo, Cg), lambda i, g: (jnp.maximum(i * hb - 1, 0), g)),
                  pl.BlockSpec((1, Cg, Cg), lambda i, g: (g, 0, 0)),
                  pl.BlockSpec((1, Cg), lambda i, g: (0, g))],
        out_specs=pl.BlockSpec((tm, Cg), lambda i, g: (i, g)),
        out_shape=jax.ShapeDtypeStruct((T, G * Cg), BF16),
        compiler_params=_params(("parallel", "parallel")),
        name="pool",
    )(proj, proj, pool_w_bf, pool_scale.reshape(1, G * Cg))


SOLVE_BASE_LOG2 = 3


def _bdot(a, b):
    return jnp.dot(a.astype(BF16), b.astype(BF16), preferred_element_type=F32)


def _solve_unit_lower(L, X, ri, ci):
    b = SOLVE_BASE_LOG2
    A = jnp.where((ri >> b) == (ci >> b), L, 0.0)
    A2 = _bdot(A, A)
    yield
    A3 = _bdot(A, A2)
    A4 = _bdot(A2, A2)
    yield
    T0 = A2 - A - A3
    Tm = T0 + A4 + _bdot(T0, A4)
    yield
    lvl = b
    while (1 << lvl) < CHUNK:
        Cm = jnp.where(((ri >> (lvl + 1)) == (ci >> (lvl + 1))) & ((ri >> lvl) != (ci >> lvl)), L, 0.0)
        M = Cm + _bdot(Cm, Tm)
        yield
        Tm = Tm - M - _bdot(Tm, M)
        yield
        lvl += 1
    return X + _bdot(Tm, X)


def _round_robin(gens):
    out = [None] * len(gens)
    live = list(range(len(gens)))
    while live:
        for j in list(live):
            try:
                next(gens[j])
            except StopIteration as e:
                out[j] = e.value
                live.remove(j)
    return out


def _delta_kernel(q_ref, k_ref, v_ref, z_ref, ab_ref, wq_ref, wk_ref, wv_ref, nega_ref, dtb_ref,
                  hng_ref, o_ref, s_scr, halo_scr, qs_scr, ks_scr, vs_scr, g_scr, gt_scr, beta_scr,
                  *, hb, rows, n_heads):
    hg = pl.program_id(1)
    cb = pl.program_id(2)

    @pl.when(cb == 0)
    def _():
        s_scr[...] = jnp.zeros_like(s_scr)
        halo_scr[...] = jnp.zeros_like(halo_scr)

    def conv(idx, x_ref, w_ref):
        x = x_ref[...].astype(F32)
        ext = jnp.concatenate([halo_scr[idx], x], axis=0)
        w = w_ref[...]
        y = x * w[CONV_WIDTH - 1:CONV_WIDTH]
        for s in range(1, CONV_WIDTH):
            y = y + ext[8 - s:8 - s + rows] * w[CONV_WIDTH - 1 - s:CONV_WIDTH - s]
        halo_scr[idx] = x[rows - 8:rows]
        return _silu(y)

    def l2n(t, mult):
        outs = []
        for j in range(hb):
            th = t[:, j * HEAD_DIM:(j + 1) * HEAD_DIM]
            outs.append(th * (lax.rsqrt(jnp.sum(th * th, axis=-1, keepdims=True) + EPS) * mult))
        return jnp.concatenate(outs, axis=1) if hb > 1 else outs[0]

    qs_scr[...] = l2n(conv(0, q_ref, wq_ref), HEAD_DIM ** -0.5)
    ks_scr[...] = l2n(conv(1, k_ref, wk_ref), 1.0)
    vs_scr[...] = conv(2, v_ref, wv_ref)

    ri = lax.broadcasted_iota(jnp.int32, (CHUNK, CHUNK), 0)
    ci = lax.broadcasted_iota(jnp.int32, (CHUNK, CHUNK), 1)
    causal = ri >= ci
    strict = ri > ci
    tri = causal.astype(F32)

    ab = ab_ref[...]
    ld = nega_ref[...] * jax.nn.softplus(ab + dtb_ref[...])
    beta_scr[...] = jax.nn.sigmoid(ab)
    for cc in range(rows // CHUNK):
        g = jnp.dot(tri, ld[cc * CHUNK:(cc + 1) * CHUNK], preferred_element_type=F32, precision=HIGHEST)
        g_scr[cc * CHUNK:(cc + 1) * CHUNK, :] = g
        gt_scr[cc] = g.T

    lane = lax.broadcasted_iota(jnp.int32, (CHUNK, LANES), 1)

    hng = hng_ref[...]

    def chunk_head(h, q, k, v, z, S, gch, g_row, btc):
        gb = jnp.broadcast_to(jnp.sum(jnp.where(lane == h, gch, 0.0), axis=-1, keepdims=True),
                              (CHUNK, LANES))
        beta_b = jnp.sum(jnp.where(lane == n_heads + h, btc, 0.0), axis=-1, keepdims=True)
        diff = gb[:, :CHUNK] - g_row
        decay = jnp.where(causal, jnp.exp(jnp.where(causal, diff, 0.0)), 0.0)
        eg = jnp.exp(gb)
        g_last = gb[CHUNK - 1:CHUNK, :]
        ew = jnp.exp(g_last - gb)
        e_last = jnp.exp(g_last)

        kb = k * beta_b
        kq = lax.dot_general(jnp.concatenate([kb, q], axis=0).astype(BF16), k.astype(BF16),
                             (((1,), (1,)), ((), ())), preferred_element_type=F32)
        yield
        L = jnp.where(strict, kq[:CHUNK] * decay, 0.0)
        intra = kq[CHUNK:] * decay
        X = jnp.concatenate([v * beta_b, kb * eg], axis=1)
        X = yield from _solve_unit_lower(L, X, ri, ci)
        yield
        Xb = X.astype(BF16)
        iv = jnp.dot(intra.astype(BF16), Xb, preferred_element_type=F32)
        kv = lax.dot_general((k * ew).astype(BF16), Xb, (((0,), (0,)), ((), ())),
                             preferred_element_type=F32)
        yield
        qe = q * eg - iv[:, HEAD_DIM:]
        lhs = jnp.concatenate([kv[:, HEAD_DIM:], qe], axis=0).astype(BF16)
        rs = jnp.dot(lhs, S.astype(BF16), preferred_element_type=F32)
        yield
        s_new = e_last * S - rs[:HEAD_DIM] + kv[:, :HEAD_DIM]
        o = rs[HEAD_DIM:] + iv[:, :HEAD_DIM]
        o = o * lax.rsqrt(jnp.mean(o * o, axis=-1, keepdims=True) + EPS) * hng
        return s_new, (o * _silu(z)).astype(o_ref.dtype)

    def chunk_body(c, carry):
        rsl = pl.ds(pl.multiple_of(c * CHUNK, CHUNK), CHUNK)
        gch = g_scr[rsl, :]
        btc = beta_scr[rsl, :]
        ins = []
        for j in range(hb):
            sl = slice(j * HEAD_DIM, (j + 1) * HEAD_DIM)
            ins.append((qs_scr[rsl, sl], ks_scr[rsl, sl], vs_scr[rsl, sl],
                        z_ref[rsl, sl].astype(F32), s_scr[j]))
        outs = _round_robin([chunk_head(hg * hb + j, *ins[j], gch, gt_scr[c, pl.ds(hg * hb + j, 1), :], btc)
                             for j in range(hb)])
        for j in range(hb):
            s_scr[j] = outs[j][0]
            o_ref[rsl, j * HEAD_DIM:(j + 1) * HEAD_DIM] = outs[j][1]
        return carry

    lax.fori_loop(0, rows // CHUNK, chunk_body, 0)


def _delta(proj, ab, conv_w, a_log, dt_bias, head_norm_g, *, batch, seq, n_heads, col0):
    T = proj.shape[0]
    KW = n_heads * HEAD_DIM
    hb = next(d for d in (16, 8, 4, 2, 1) if n_heads % d == 0)
    C = hb * HEAD_DIM
    rows = _tile(seq, 256)
    nb = seq // rows
    assert col0 % C == 0 and KW % C == 0 and n_heads <= LANES // 2
    qb, kb_, vb, zb = ((col0 + s * KW) // C for s in range(4))
    row = lambda b, g, c: b * nb + c
    pad = lambda t: jnp.zeros((1, LANES), F32).at[0, :n_heads].set(t)
    nega = pad(-jnp.exp(a_log.astype(F32)))
    dtb = pad(dt_bias.astype(F32))
    wq, wk, wv = (conv_w[:, s * KW:(s + 1) * KW] for s in range(3))
    kern = functools.partial(_delta_kernel, hb=hb, rows=rows, n_heads=n_heads)
    wspec = pl.BlockSpec((CONV_WIDTH, C), lambda b, g, c: (0, g))
    vec = pl.BlockSpec((1, LANES), lambda b, g, c: (0, 0))
    return pl.pallas_call(
        kern,
        grid=(batch, n_heads // hb, nb),
        in_specs=[pl.BlockSpec((rows, C), lambda b, g, c: (row(b, g, c), qb + g)),
                  pl.BlockSpec((rows, C), lambda b, g, c: (row(b, g, c), kb_ + g)),
                  pl.BlockSpec((rows, C), lambda b, g, c: (row(b, g, c), vb + g)),
                  pl.BlockSpec((rows, C), lambda b, g, c: (row(b, g, c), zb + g)),
                  pl.BlockSpec((rows, LANES), lambda b, g, c: (row(b, g, c), 0)),
                  wspec, wspec, wspec, vec, vec, vec],
        out_specs=pl.BlockSpec((rows, C), lambda b, g, c: (row(b, g, c), g)),
        out_shape=jax.ShapeDtypeStruct((T, KW), BF16),
        scratch_shapes=[pltpu.VMEM((hb, HEAD_DIM, HEAD_DIM), F32),
                        pltpu.VMEM((3, 8, C), F32),
                        pltpu.VMEM((rows, C), F32),
                        pltpu.VMEM((rows, C), F32),
                        pltpu.VMEM((rows, C), F32),
                        pltpu.VMEM((rows, LANES), F32),
                        pltpu.VMEM((rows // CHUNK, LANES, CHUNK), F32),
                        pltpu.VMEM((rows, LANES), F32)],
        compiler_params=_params(("parallel", "parallel", "arbitrary")),
        name="delta",
    )(proj, proj, proj, proj, ab, wq, wk, wv, nega, dtb, head_norm_g.reshape(1, HEAD_DIM).astype(F32))


def _merge_kernel(a1_ref, a2_ref, w1_ref, w2_ref, g1_ref, g2_ref, src_ref, o_ref, dst_ref):
    y1 = jnp.dot(a1_ref[...], w1_ref[...], preferred_element_type=F32)
    y2 = jnp.dot(a2_ref[...], w2_ref[...], preferred_element_type=F32)
    m = jax.nn.sigmoid(g1_ref[...].astype(F32)) * y1 + jax.nn.sigmoid(g2_ref[...].astype(F32)) * y2
    o_ref[...] = m.astype(o_ref.dtype)
    dst_ref[...] = src_ref[...].astype(dst_ref.dtype)


def _merge(yp, yd, w1, w2, gates, cast_along):
    T, K1 = yp.shape
    K2 = yd.shape[1]
    D = w1.shape[1]
    tm, tn = _tile(T, 1024), _tile(D, 512)
    ni, nj = T // tm, D // tn
    rows = cast_along.shape[0] // (ni * nj)
    assert rows * ni * nj == cast_along.shape[0] and rows % 16 == 0
    side = pl.BlockSpec((rows, cast_along.shape[1]), lambda i, j: (i * nj + j, 0))
    return pl.pallas_call(
        _merge_kernel,
        grid=(ni, nj),
        in_specs=[pl.BlockSpec((tm, K1), lambda i, j: (i, 0)),
                  pl.BlockSpec((tm, K2), lambda i, j: (i, 0)),
                  pl.BlockSpec((K1, tn), lambda i, j: (0, j)),
                  pl.BlockSpec((K2, tn), lambda i, j: (0, j)),
                  pl.BlockSpec((tm, tn), lambda i, j: (i, j)),
                  pl.BlockSpec((tm, tn), lambda i, j: (i, nj + j)),
                  side],
        out_specs=[pl.BlockSpec((tm, tn), lambda i, j: (i, j)), side],
        out_shape=[jax.ShapeDtypeStruct((T, D), BF16), jax.ShapeDtypeStruct(cast_along.shape, BF16)],
        compiler_params=_params(("arbitrary", "arbitrary")),
        name="merge",
    )(yp, yd, w1, w2, gates, gates, cast_along)


def _outproj_kernel(a_ref, w_ref, x_ref, gate_ref, o_ref):
    acc = jnp.dot(a_ref[...], w_ref[...], preferred_element_type=F32)
    o_ref[...] = x_ref[...] + gate_ref[0] * acc


def _outproj(a, w, x2, gate, seq):
    T, K = a.shape
    D = w.shape[1]
    tm, tn = _tile(seq, 1024), _tile(D, 1024)
    return pl.pallas_call(
        _outproj_kernel,
        grid=(T // tm, D // tn),
        in_specs=[pl.BlockSpec((tm, K), lambda i, j: (i, 0)),
                  pl.BlockSpec((K, tn), lambda i, j: (0, j)),
                  pl.BlockSpec((tm, tn), lambda i, j: (i, j)),
                  pl.BlockSpec((1, 1, tn), lambda i, j: ((i * tm) // seq, 0, j))],
        out_specs=pl.BlockSpec((tm, tn), lambda i, j: (i, j)),
        out_shape=jax.ShapeDtypeStruct((T, D), F32),
        compiler_params=_params(("parallel", "parallel")),
        name="outproj",
    )(a, w, x2, gate)


def _ffn_kernel(u_ref, w1_ref, w2_ref, h_hbm, gate_ref, fg_ref, o_hbm, acc, hbuf, hsem, osem, *, nf, tn):
    i = pl.program_id(0)
    f = pl.program_id(1)
    tm = acc.shape[0]
    rows = hbuf.shape[1]
    nchunk = tm // rows

    def h_copy(r, slot):
        return pltpu.make_async_copy(h_hbm.at[pl.ds(i * tm + r * rows, rows), :], hbuf.at[slot], hsem.at[slot])

    def o_copy(r):
        return pltpu.make_async_copy(acc.at[pl.ds(r * rows, rows), :],
                                     o_hbm.at[pl.ds(i * tm + r * rows, rows), :], osem.at[r])

    @pl.when(f == 0)
    def _():
        acc[...] = jnp.zeros_like(acc)

    @pl.when(f == nf - 1)
    def _():
        h_copy(0, 0).start()

    hdn = jnp.maximum(jnp.dot(u_ref[...], w1_ref[...], preferred_element_type=F32), 0.0)
    hb = (hdn * hdn).astype(BF16)
    for n0 in range(0, acc.shape[1], tn):
        acc[:, n0:n0 + tn] += jnp.dot(hb, w2_ref[:, n0:n0 + tn], preferred_element_type=F32)

    @pl.when(f == nf - 1)
    def _():
        def body(r, carry):
            slot = lax.rem(r, 2)

            @pl.when(r + 1 < nchunk)
            def _():
                h_copy(r + 1, 1 - slot).start()

            h_copy(r, slot).wait()
            sl = pl.ds(pl.multiple_of(r * rows, rows), rows)
            h2 = hbuf[slot] + gate_ref[0] * acc[sl, :]
            acc[sl, :] = h2 * lax.rsqrt(jnp.mean(h2 * h2, axis=-1, keepdims=True) + EPS) * fg_ref[...]
            o_copy(r).start()
            return carry

        lax.fori_loop(0, nchunk, body, 0)
        for r in range(nchunk):
            o_copy(r).wait()


def _ffn(u2, w1, w2, h, gate, final_g, seq):
    T, D = u2.shape
    F = w1.shape[1]
    tm, tf = _tile(seq, FFN_TM), _tile(F, FFN_TF)
    nf = F // tf
    rows = _tile(tm, FFN_EPILOGUE_ROWS)
    return pl.pallas_call(
        functools.partial(_ffn_kernel, nf=nf, tn=_tile(D, 1024)),
        grid=(T // tm, nf),
        in_specs=[pl.BlockSpec((tm, D), lambda i, f: (i, 0)),
                  pl.BlockSpec((D, tf), lambda i, f: (0, f)),
                  pl.BlockSpec((tf, D), lambda i, f: (f, 0)),
                  pl.BlockSpec(memory_space=pl.ANY),
                  pl.BlockSpec((1, 1, D), lambda i, f: ((i * tm) // seq, 0, 0)),
                  pl.BlockSpec((1, D), lambda i, f: (0, 0))],
        out_specs=pl.BlockSpec(memory_space=pl.ANY),
        out_shape=jax.ShapeDtypeStruct((T, D), F32),
        scratch_shapes=[pltpu.VMEM((tm, D), F32),
                        pltpu.VMEM((2, rows, D), F32),
                        pltpu.SemaphoreType.DMA((2,)),
                        pltpu.SemaphoreType.DMA((tm // rows,))],
        compiler_params=pltpu.CompilerParams(dimension_semantics=("arbitrary", "arbitrary"),
                                             vmem_limit_bytes=FFN_VMEM_LIMIT),
        name="ffn",
    )(u2, w1, w2, h, gate, final_g.reshape(1, D))


def kernel(x, c, w_ada, b_ada, norm1_g, w_in, conv_w, pool_w, pool_scale, a_log, dt_bias,
           head_norm_g, w_up_pool, w_up_delta, w_out, norm2_g, w_ff_in, w_ff_out, final_g):
    B, S, D = x.shape
    T = B * S
    H = a_log.shape[0]
    KW = H * HEAD_DIM
    PW = pool_scale.shape[0]
    assert w_in.shape[1] == PW + 3 * KW + KW + 2 * H + 2 * D
    assert w_ada.shape[1] == N_MOD * D and S % CHUNK == 0 and 2 * H <= LANES

    mod = _ada(c, w_ada, b_ada)
    shift1, scale1, gate1, shift2, scale2, gate2 = (
        mod[:, i * D:(i + 1) * D].reshape(B, 1, D) for i in range(N_MOD))

    c_ab = PW + 4 * KW
    w_bf = w_in.astype(BF16)
    w_gates = w_bf[:, c_ab + 2 * H:]
    w_ab = jnp.pad(w_bf[:, c_ab:c_ab + 2 * H], ((0, 0), (0, LANES - 2 * H)))

    x2 = x.reshape(T, D)
    u = _norm_mod(x2, norm1_g, shift1, scale1, S)
    proj, ab, _ = _matmul(u, w_bf, BF16, n_cols=c_ab, name="inproj", narrow=w_ab)
    G, Cg, _ = pool_w.shape
    gates, _, (w_up_pool, w_up_delta, w_out, w_ff_in, pool_w) = _matmul(
        u, w_gates, BF16, name="inproj_gates",
        cast_along=(w_up_pool, w_up_delta, w_out, w_ff_in, pool_w.reshape(G * Cg, Cg)))

    yp = _pool(proj, pool_w.reshape(G, Cg, Cg), pool_scale, S)
    yd = _delta(proj, ab, conv_w, a_log, dt_bias, head_norm_g,
                batch=B, seq=S, n_heads=H, col0=PW)
    merged, w_ff_out = _merge(yp, yd, w_up_pool, w_up_delta, gates, cast_along=w_ff_out)
    h = _outproj(merged, w_out, x2, gate1, S)

    u2 = _norm_mod(h, norm2_g, shift2, scale2, S)
    out = _ffn(u2, w_ff_in, w_ff_out, h, gate2, final_g, S)
    return out.reshape(B, S, D)
```

```python
import functools

import jax
import jax.numpy as jnp
from jax import lax
from jax.experimental import pallas as pl
from jax.experimental.pallas import tpu as pltpu

EPS = 1e-6
HEAD_DIM = 128
CHUNK = 64
CONV_WIDTH = 4
POOL_WINDOWS = (2, 4, 8, 16)
N_MOD = 6
LANES = 128
VMEM_LIMIT = 56 * 1024 * 1024
FFN_TM, FFN_TF = 1024, 512
FFN_EPILOGUE_ROWS = 128
FFN_VMEM_LIMIT = 60 * 1024 * 1024
F32 = jnp.float32
BF16 = jnp.bfloat16
HIGHEST = lax.Precision.HIGHEST


def _params(sem):
    return pltpu.CompilerParams(dimension_semantics=sem, vmem_limit_bytes=VMEM_LIMIT)


def _tile(n, pref):
    t = min(n, pref)
    while n % t:
        t //= 2
    return t


def _silu(x):
    return x * jax.nn.sigmoid(x)


def _ada_kernel(c_ref, w_ref, b_ref, o_ref):
    s = _silu(c_ref[...]).astype(BF16)
    o_ref[...] = jnp.dot(s, w_ref[...].astype(BF16), preferred_element_type=F32) + b_ref[...]


def _ada(c, w_ada, b_ada):
    B, D = c.shape
    N = w_ada.shape[1]
    rows = 8
    cp = jnp.zeros((rows, D), F32).at[:B].set(c)
    tn = _tile(N, 512)
    out = pl.pallas_call(
        _ada_kernel,
        grid=(N // tn,),
        in_specs=[pl.BlockSpec((rows, D), lambda j: (0, 0)),
                  pl.BlockSpec((D, tn), lambda j: (0, j)),
                  pl.BlockSpec((1, tn), lambda j: (0, j))],
        out_specs=pl.BlockSpec((rows, tn), lambda j: (0, j)),
        out_shape=jax.ShapeDtypeStruct((rows, N), F32),
        compiler_params=_params(("parallel",)),
        name="ada",
    )(cp, w_ada, b_ada.reshape(1, N))
    return out[:B]


def _norm_kernel(x_ref, g_ref, shift_ref, scale_ref, o_ref):
    x = x_ref[...]
    n = x * lax.rsqrt(jnp.mean(x * x, axis=-1, keepdims=True) + EPS) * g_ref[...]
    o_ref[...] = (n * (1.0 + scale_ref[0]) + shift_ref[0]).astype(o_ref.dtype)


def _norm_mod(x2, g, shift, scale, seq):
    T, D = x2.shape
    tm = _tile(seq, 512)
    bmap = lambda i: ((i * tm) // seq, 0, 0)
    return pl.pallas_call(
        _norm_kernel,
        grid=(T // tm,),
        in_specs=[pl.BlockSpec((tm, D), lambda i: (i, 0)),
                  pl.BlockSpec((1, D), lambda i: (0, 0)),
                  pl.BlockSpec((1, 1, D), bmap),
                  pl.BlockSpec((1, 1, D), bmap)],
        out_specs=pl.BlockSpec((tm, D), lambda i: (i, 0)),
        out_shape=jax.ShapeDtypeStruct((T, D), BF16),
        compiler_params=_params(("parallel",)),
        name="norm_mod",
    )(x2, g.reshape(1, D), shift, scale)


def _mm_kernel(a_ref, w_ref, *refs, n_narrow):
    n_side = (len(refs) - 1 - 2 * n_narrow) // 2
    ins, outs = refs[:n_narrow + n_side], refs[n_narrow + n_side:]
    o_ref = outs[0]
    o_ref[...] = jnp.dot(a_ref[...], w_ref[...], preferred_element_type=F32).astype(o_ref.dtype)
    if n_narrow:
        @pl.when(pl.program_id(1) == 0)
        def _():
            outs[1][...] = jnp.dot(a_ref[...], ins[0][...], preferred_element_type=F32)
    for src, dst in zip(ins[n_narrow:], outs[1 + n_narrow:]):
        dst[...] = src[...].astype(dst.dtype)


def _matmul(a, w, out_dtype, n_cols=None, tm_pref=1024, tn_pref=1024, name="mm", narrow=None, cast_along=()):
    M, K = a.shape
    N = w.shape[1] if n_cols is None else n_cols
    tm, tn = _tile(M, tm_pref), _tile(N, tn_pref)
    ni, nj = M // tm, N // tn
    steps = ni * nj
    side_specs = []
    for t in cast_along:
        rows = t.shape[0] // steps
        assert rows * steps == t.shape[0] and rows % 16 == 0, (t.shape, steps)
        side_specs.append(pl.BlockSpec((rows, t.shape[1]), lambda i, j: (i * nj + j, 0)))
    nar_in = [] if narrow is None else [pl.BlockSpec(narrow.shape, lambda i, j: (0, 0))]
    nar_out = [] if narrow is None else [pl.BlockSpec((tm, narrow.shape[1]), lambda i, j: (i, 0))]
    nar_shape = [] if narrow is None else [jax.ShapeDtypeStruct((M, narrow.shape[1]), F32)]
    outs = pl.pallas_call(
        functools.partial(_mm_kernel, n_narrow=len(nar_in)),
        grid=(ni, nj),
        in_specs=[pl.BlockSpec((tm, K), lambda i, j: (i, 0)),
                  pl.BlockSpec((K, tn), lambda i, j: (0, j))] + nar_in + side_specs,
        out_specs=[pl.BlockSpec((tm, tn), lambda i, j: (i, j))] + nar_out + side_specs,
        out_shape=[jax.ShapeDtypeStruct((M, N), out_dtype)] + nar_shape
                  + [jax.ShapeDtypeStruct(t.shape, BF16) for t in cast_along],
        compiler_params=_params(("arbitrary", "arbitrary")),
        name=name,
    )(a, w, *([] if narrow is None else [narrow]), *cast_along)
    return outs[0], (None if narrow is None else outs[1]), outs[1 + len(nar_in):]


def _pool_kernel(p_ref, halo_ref, w_ref, scale_ref, o_ref, *, seq, tm, halo):
    i = pl.program_id(0)
    g = pl.program_id(1)
    start = (i * tm) % seq
    p = p_ref[...]
    hl = jnp.where(start > 0, halo_ref[...], jnp.zeros_like(halo_ref))
    ext = jnp.concatenate([jnp.zeros((LANES - halo, p.shape[1]), p.dtype), hl, p], axis=0)
    win = jnp.where(g == 0, POOL_WINDOWS[0],
                    jnp.where(g == 1, POOL_WINDOWS[1],
                              jnp.where(g == 2, POOL_WINDOWS[2], POOL_WINDOWS[3])))
    r = lax.broadcasted_iota(jnp.int32, (tm, tm + LANES), 0)
    cidx = lax.broadcasted_iota(jnp.int32, (tm, tm + LANES), 1)
    d = r + LANES - cidx
    band = jnp.where((d >= 0) & (d < win), 1.0, 0.0).astype(BF16)
    wsum = jnp.dot(band, ext, preferred_element_type=F32)
    pos = start + lax.broadcasted_iota(jnp.int32, (tm, 1), 0) + 1
    count = jnp.minimum(pos, win).astype(F32)
    mixed = wsum / count - p.astype(F32)
    y = jnp.dot(mixed.astype(BF16), w_ref[0], preferred_element_type=F32)
    o_ref[...] = (y * scale_ref[...]).astype(o_ref.dtype)


def _pool(proj, pool_w_bf, pool_scale, seq):
    T = proj.shape[0]
    G, Cg, _ = pool_w_bf.shape
    assert G == len(POOL_WINDOWS)
    halo = 16
    tm = _tile(seq, 512)
    hb = tm // halo
    kern = functools.partial(_pool_kernel, seq=seq, tm=tm, halo=halo)
    return pl.pallas_call(
        kern,
        grid=(T // tm, G),
        in_specs=[pl.BlockSpec((tm, Cg), lambda i, g: (i, g)),
                  pl.BlockSpec((halo, Cg), lambda i, g: (jnp.maximum(i * hb - 1, 0), g)),
                  pl.BlockSpec((1, Cg, Cg), lambda i, g: (g, 0, 0)),
                  pl.BlockSpec((1, Cg), lambda i, g: (0, g))],
        out_specs=pl.BlockSpec((tm, Cg), lambda i, g: (i, g)),
        out_shape=jax.ShapeDtypeStruct((T, G * Cg), BF16),
        compiler_params=_params(("parallel", "parallel")),
        name="pool",
    )(proj, proj, pool_w_bf, pool_scale.reshape(1, G * Cg))


SOLVE_BASE_LOG2 = 3
CHUNKS_PER_ITER = 2


def _bdot(a, b):
    return jnp.dot(a.astype(BF16), b.astype(BF16), preferred_element_type=F32)


def _solve_unit_lower(L, X, ri, ci):
    b = SOLVE_BASE_LOG2
    A = jnp.where((ri >> b) == (ci >> b), L, 0.0)
    A2 = _bdot(A, A)
    yield
    A3 = _bdot(A, A2)
    A4 = _bdot(A2, A2)
    yield
    T0 = A2 - A - A3
    Tm = T0 + A4 + _bdot(T0, A4)
    yield
    lvl = b
    while (1 << lvl) < CHUNK:
        Cm = jnp.where(((ri >> (lvl + 1)) == (ci >> (lvl + 1))) & ((ri >> lvl) != (ci >> lvl)), L, 0.0)
        M = Cm + _bdot(Cm, Tm)
        yield
        Tm = Tm - M - _bdot(Tm, M)
        yield
        lvl += 1
    return X + _bdot(Tm, X)


def _round_robin(gens, on_done):
    live = list(range(len(gens)))
    while live:
        for n in list(live):
            try:
                next(gens[n])
            except StopIteration as e:
                on_done(n, e.value)
                live.remove(n)


def _delta_kernel(q_ref, k_ref, v_ref, z_ref, ab_ref, wq_ref, wk_ref, wv_ref, nega_ref, dtb_ref,
                  hng_ref, o_ref, s_scr, halo_scr, qs_scr, ks_scr, vs_scr, g_scr, gt_scr, beta_scr,
                  *, hb, rows, n_heads):
    hg = pl.program_id(1)
    cb = pl.program_id(2)

    @pl.when(cb == 0)
    def _():
        s_scr[...] = jnp.zeros_like(s_scr)
        halo_scr[...] = jnp.zeros_like(halo_scr)

    def conv(idx, x_ref, w_ref, dst, mult):
        for j in range(hb):
            cs = slice(j * HEAD_DIM, (j + 1) * HEAD_DIM)
            w = w_ref[:, cs]
            for r0 in range(0, rows, CHUNK):
                x = x_ref[r0:r0 + CHUNK, cs].astype(F32)
                prev = halo_scr[idx, :, cs] if r0 == 0 else x_ref[r0 - 16:r0, cs].astype(F32)[8:]
                ext = jnp.concatenate([prev, x], axis=0)
                y = x * w[CONV_WIDTH - 1:CONV_WIDTH]
                for s in range(1, CONV_WIDTH):
                    y = y + ext[8 - s:8 - s + CHUNK] * w[CONV_WIDTH - 1 - s:CONV_WIDTH - s]
                y = _silu(y)
                if mult is not None:
                    y = y * (lax.rsqrt(jnp.sum(y * y, axis=-1, keepdims=True) + EPS) * mult)
                dst[r0:r0 + CHUNK, cs] = y
        halo_scr[idx] = x_ref[rows - 16:rows, :].astype(F32)[8:]

    conv(0, q_ref, wq_ref, qs_scr, HEAD_DIM ** -0.5)
    conv(1, k_ref, wk_ref, ks_scr, 1.0)
    conv(2, v_ref, wv_ref, vs_scr, None)

    ri = lax.broadcasted_iota(jnp.int32, (CHUNK, CHUNK), 0)
    ci = lax.broadcasted_iota(jnp.int32, (CHUNK, CHUNK), 1)
    causal = ri >= ci
    strict = ri > ci
    tri = causal.astype(F32)

    ab = ab_ref[...]
    ld = nega_ref[...] * jax.nn.softplus(ab + dtb_ref[...])
    beta_scr[...] = jax.nn.sigmoid(ab)
    for cc in range(rows // CHUNK):
        g = jnp.dot(tri, ld[cc * CHUNK:(cc + 1) * CHUNK], preferred_element_type=F32, precision=HIGHEST)
        g_scr[cc * CHUNK:(cc + 1) * CHUNK, :] = g
        gt_scr[cc] = g.T

    lane = lax.broadcasted_iota(jnp.int32, (CHUNK, LANES), 1)

    hng = hng_ref[...]

    def chunk_head(h, q, k, v, z, get_state, gch, g_row, btc):
        gb = jnp.broadcast_to(jnp.sum(jnp.where(lane == h, gch, 0.0), axis=-1, keepdims=True),
                              (CHUNK, LANES))
        beta_b = jnp.sum(jnp.where(lane == n_heads + h, btc, 0.0), axis=-1, keepdims=True)
        diff = gb[:, :CHUNK] - g_row
        decay = jnp.where(causal, jnp.exp(jnp.where(causal, diff, 0.0)), 0.0)
        eg = jnp.exp(gb)
        g_last = gb[CHUNK - 1:CHUNK, :]
        ew = jnp.exp(g_last - gb)
        e_last = jnp.exp(g_last)

        kb = k * beta_b
        kq = lax.dot_general(jnp.concatenate([kb, q], axis=0).astype(BF16), k.astype(BF16),
                             (((1,), (1,)), ((), ())), preferred_element_type=F32)
        yield
        L = jnp.where(strict, kq[:CHUNK] * decay, 0.0)
        intra = kq[CHUNK:] * decay
        X = jnp.concatenate([v * beta_b, kb * eg], axis=1)
        X = yield from _solve_unit_lower(L, X, ri, ci)
        yield
        Xb = X.astype(BF16)
        iv = jnp.dot(intra.astype(BF16), Xb, preferred_element_type=F32)
        kv = lax.dot_general((k * ew).astype(BF16), Xb, (((0,), (0,)), ((), ())),
                             preferred_element_type=F32)
        yield
        qe = q * eg - iv[:, HEAD_DIM:]
        lhs = jnp.concatenate([kv[:, HEAD_DIM:], qe], axis=0).astype(BF16)
        S = get_state()
        rs = jnp.dot(lhs, S.astype(BF16), preferred_element_type=F32)
        s_new = e_last * S - rs[:HEAD_DIM] + kv[:, :HEAD_DIM]
        o = rs[HEAD_DIM:] + iv[:, :HEAD_DIM]
        o = o * lax.rsqrt(jnp.mean(o * o, axis=-1, keepdims=True) + EPS) * hng
        return s_new, (o * _silu(z)).astype(o_ref.dtype)

    def chunk_body(cc, carry):
        gens, state = [], {}
        for u in range(CHUNKS_PER_ITER):
            c = cc * CHUNKS_PER_ITER + u
            rsl = pl.ds(pl.multiple_of(c * CHUNK, CHUNK), CHUNK)
            gch = g_scr[rsl, :]
            btc = beta_scr[rsl, :]
            for j in range(hb):
                sl = slice(j * HEAD_DIM, (j + 1) * HEAD_DIM)
                if u == 0:
                    state[j] = s_scr[j]
                gens.append(chunk_head(hg * hb + j, qs_scr[rsl, sl], ks_scr[rsl, sl], vs_scr[rsl, sl],
                                       z_ref[rsl, sl].astype(F32), functools.partial(state.get, j),
                                       gch, gt_scr[c, pl.ds(hg * hb + j, 1), :], btc))
        outs = [None] * len(gens)

        def done(n, value):
            outs[n] = value[1]
            state[n % hb] = value[0]

        _round_robin(gens, done)
        for u in range(CHUNKS_PER_ITER):
            rsl = pl.ds(pl.multiple_of((cc * CHUNKS_PER_ITER + u) * CHUNK, CHUNK), CHUNK)
            for j in range(hb):
                o_ref[rsl, j * HEAD_DIM:(j + 1) * HEAD_DIM] = outs[u * hb + j]
        for j in range(hb):
            s_scr[j] = state[j]
        return carry

    lax.fori_loop(0, rows // (CHUNK * CHUNKS_PER_ITER), chunk_body, 0)


def _delta(proj, ab, conv_w, a_log, dt_bias, head_norm_g, *, batch, seq, n_heads, col0):
    T = proj.shape[0]
    KW = n_heads * HEAD_DIM
    hb = next(d for d in (16, 8, 4, 2, 1) if n_heads % d == 0)
    C = hb * HEAD_DIM
    rows = _tile(seq, 256)
    nb = seq // rows
    assert col0 % C == 0 and KW % C == 0 and n_heads <= LANES // 2 and rows % (CHUNK * CHUNKS_PER_ITER) == 0
    qb, kb_, vb, zb = ((col0 + s * KW) // C for s in range(4))
    row = lambda b, g, c: b * nb + c
    pad = lambda t: jnp.zeros((1, LANES), F32).at[0, :n_heads].set(t)
    nega = pad(-jnp.exp(a_log.astype(F32)))
    dtb = pad(dt_bias.astype(F32))
    wq, wk, wv = (conv_w[:, s * KW:(s + 1) * KW] for s in range(3))
    kern = functools.partial(_delta_kernel, hb=hb, rows=rows, n_heads=n_heads)
    wspec = pl.BlockSpec((CONV_WIDTH, C), lambda b, g, c: (0, g))
    vec = pl.BlockSpec((1, LANES), lambda b, g, c: (0, 0))
    return pl.pallas_call(
        kern,
        grid=(batch, n_heads // hb, nb),
        in_specs=[pl.BlockSpec((rows, C), lambda b, g, c: (row(b, g, c), qb + g)),
                  pl.BlockSpec((rows, C), lambda b, g, c: (row(b, g, c), kb_ + g)),
                  pl.BlockSpec((rows, C), lambda b, g, c: (row(b, g, c), vb + g)),
                  pl.BlockSpec((rows, C), lambda b, g, c: (row(b, g, c), zb + g)),
                  pl.BlockSpec((rows, LANES), lambda b, g, c: (row(b, g, c), 0)),
                  wspec, wspec, wspec, vec, vec, vec],
        out_specs=pl.BlockSpec((rows, C), lambda b, g, c: (row(b, g, c), g)),
        out_shape=jax.ShapeDtypeStruct((T, KW), BF16),
        scratch_shapes=[pltpu.VMEM((hb, HEAD_DIM, HEAD_DIM), F32),
                        pltpu.VMEM((3, 8, C), F32),
                        pltpu.VMEM((rows, C), F32),
                        pltpu.VMEM((rows, C), F32),
                        pltpu.VMEM((rows, C), F32),
                        pltpu.VMEM((rows, LANES), F32),
                        pltpu.VMEM((rows // CHUNK, LANES, CHUNK), F32),
                        pltpu.VMEM((rows, LANES), F32)],
        compiler_params=_params(("parallel", "parallel", "arbitrary")),
        name="delta",
    )(proj, proj, proj, proj, ab, wq, wk, wv, nega, dtb, head_norm_g.reshape(1, HEAD_DIM).astype(F32))


def _merge_kernel(a1_ref, a2_ref, w1_ref, w2_ref, g1_ref, g2_ref, src_ref, o_ref, dst_ref):
    y1 = jnp.dot(a1_ref[...], w1_ref[...], preferred_element_type=F32)
    y2 = jnp.dot(a2_ref[...], w2_ref[...], preferred_element_type=F32)
    m = jax.nn.sigmoid(g1_ref[...].astype(F32)) * y1 + jax.nn.sigmoid(g2_ref[...].astype(F32)) * y2
    o_ref[...] = m.astype(o_ref.dtype)
    dst_ref[...] = src_ref[...].astype(dst_ref.dtype)


def _merge(yp, yd, w1, w2, gates, cast_along):
    T, K1 = yp.shape
    K2 = yd.shape[1]
    D = w1.shape[1]
    tm, tn = _tile(T, 1024), _tile(D, 512)
    ni, nj = T // tm, D // tn
    rows = cast_along.shape[0] // (ni * nj)
    assert rows * ni * nj == cast_along.shape[0] and rows % 16 == 0
    side = pl.BlockSpec((rows, cast_along.shape[1]), lambda i, j: (i * nj + j, 0))
    return pl.pallas_call(
        _merge_kernel,
        grid=(ni, nj),
        in_specs=[pl.BlockSpec((tm, K1), lambda i, j: (i, 0)),
                  pl.BlockSpec((tm, K2), lambda i, j: (i, 0)),
                  pl.BlockSpec((K1, tn), lambda i, j: (0, j)),
                  pl.BlockSpec((K2, tn), lambda i, j: (0, j)),
                  pl.BlockSpec((tm, tn), lambda i, j: (i, j)),
                  pl.BlockSpec((tm, tn), lambda i, j: (i, nj + j)),
                  side],
        out_specs=[pl.BlockSpec((tm, tn), lambda i, j: (i, j)), side],
        out_shape=[jax.ShapeDtypeStruct((T, D), BF16), jax.ShapeDtypeStruct(cast_along.shape, BF16)],
        compiler_params=_params(("arbitrary", "arbitrary")),
        name="merge",
    )(yp, yd, w1, w2, gates, gates, cast_along)


def _outproj_kernel(a_ref, w_ref, x_ref, gate_ref, o_ref):
    acc = jnp.dot(a_ref[...], w_ref[...], preferred_element_type=F32)
    o_ref[...] = x_ref[...] + gate_ref[0] * acc


def _outproj(a, w, x2, gate, seq):
    T, K = a.shape
    D = w.shape[1]
    tm, tn = _tile(seq, 1024), _tile(D, 1024)
    return pl.pallas_call(
        _outproj_kernel,
        grid=(T // tm, D // tn),
        in_specs=[pl.BlockSpec((tm, K), lambda i, j: (i, 0)),
                  pl.BlockSpec((K, tn), lambda i, j: (0, j)),
                  pl.BlockSpec((tm, tn), lambda i, j: (i, j)),
                  pl.BlockSpec((1, 1, tn), lambda i, j: ((i * tm) // seq, 0, j))],
        out_specs=pl.BlockSpec((tm, tn), lambda i, j: (i, j)),
        out_shape=jax.ShapeDtypeStruct((T, D), F32),
        compiler_params=_params(("parallel", "parallel")),
        name="outproj",
    )(a, w, x2, gate)


def _ffn_kernel(u_ref, w1_ref, w2_ref, h_hbm, gate_ref, fg_ref, o_hbm, acc, hbuf, hsem, osem, *, nf, tn):
    i = pl.program_id(0)
    f = pl.program_id(1)
    tm = acc.shape[0]
    rows = hbuf.shape[1]
    nchunk = tm // rows

    def h_copy(r, slot):
        return pltpu.make_async_copy(h_hbm.at[pl.ds(i * tm + r * rows, rows), :], hbuf.at[slot], hsem.at[slot])

    def o_copy(r):
        return pltpu.make_async_copy(acc.at[pl.ds(r * rows, rows), :],
                                     o_hbm.at[pl.ds(i * tm + r * rows, rows), :], osem.at[r])

    @pl.when(f == 0)
    def _():
        acc[...] = jnp.zeros_like(acc)

    @pl.when(f == nf - 1)
    def _():
        h_copy(0, 0).start()

    hdn = jnp.maximum(jnp.dot(u_ref[...], w1_ref[...], preferred_element_type=F32), 0.0)
    hb = (hdn * hdn).astype(BF16)
    for n0 in range(0, acc.shape[1], tn):
        acc[:, n0:n0 + tn] += jnp.dot(hb, w2_ref[:, n0:n0 + tn], preferred_element_type=F32)

    @pl.when(f == nf - 1)
    def _():
        def body(r, carry):
            slot = lax.rem(r, 2)

            @pl.when(r + 1 < nchunk)
            def _():
                h_copy(r + 1, 1 - slot).start()

            h_copy(r, slot).wait()
            sl = pl.ds(pl.multiple_of(r * rows, rows), rows)
            h2 = hbuf[slot] + gate_ref[0] * acc[sl, :]
            acc[sl, :] = h2 * lax.rsqrt(jnp.mean(h2 * h2, axis=-1, keepdims=True) + EPS) * fg_ref[...]
            o_copy(r).start()
            return carry

        lax.fori_loop(0, nchunk, body, 0)
        for r in range(nchunk):
            o_copy(r).wait()


def _ffn(u2, w1, w2, h, gate, final_g, seq):
    T, D = u2.shape
    F = w1.shape[1]
    tm, tf = _tile(seq, FFN_TM), _tile(F, FFN_TF)
    nf = F // tf
    rows = _tile(tm, FFN_EPILOGUE_ROWS)
    return pl.pallas_call(
        functools.partial(_ffn_kernel, nf=nf, tn=_tile(D, 1024)),
        grid=(T // tm, nf),
        in_specs=[pl.BlockSpec((tm, D), lambda i, f: (i, 0)),
                  pl.BlockSpec((D, tf), lambda i, f: (0, f)),
                  pl.BlockSpec((tf, D), lambda i, f: (f, 0)),
                  pl.BlockSpec(memory_space=pl.ANY),
                  pl.BlockSpec((1, 1, D), lambda i, f: ((i * tm) // seq, 0, 0)),
                  pl.BlockSpec((1, D), lambda i, f: (0, 0))],
        out_specs=pl.BlockSpec(memory_space=pl.ANY),
        out_shape=jax.ShapeDtypeStruct((T, D), F32),
        scratch_shapes=[pltpu.VMEM((tm, D), F32),
                        pltpu.VMEM((2, rows, D), F32),
                        pltpu.SemaphoreType.DMA((2,)),
                        pltpu.SemaphoreType.DMA((tm // rows,))],
        compiler_params=pltpu.CompilerParams(dimension_semantics=("arbitrary", "arbitrary"),
                                             vmem_limit_bytes=FFN_VMEM_LIMIT),
        name="ffn",
    )(u2, w1, w2, h, gate, final_g.reshape(1, D))


def kernel(x, c, w_ada, b_ada, norm1_g, w_in, conv_w, pool_w, pool_scale, a_log, dt_bias,
           head_norm_g, w_up_pool, w_up_delta, w_out, norm2_g, w_ff_in, w_ff_out, final_g):
    B, S, D = x.shape
    T = B * S
    H = a_log.shape[0]
    KW = H * HEAD_DIM
    PW = pool_scale.shape[0]
    assert w_in.shape[1] == PW + 3 * KW + KW + 2 * H + 2 * D
    assert w_ada.shape[1] == N_MOD * D and S % CHUNK == 0 and 2 * H <= LANES

    mod = _ada(c, w_ada, b_ada)
    shift1, scale1, gate1, shift2, scale2, gate2 = (
        mod[:, i * D:(i + 1) * D].reshape(B, 1, D) for i in range(N_MOD))

    c_ab = PW + 4 * KW
    w_bf = w_in.astype(BF16)
    w_gates = w_bf[:, c_ab + 2 * H:]
    w_ab = jnp.pad(w_bf[:, c_ab:c_ab + 2 * H], ((0, 0), (0, LANES - 2 * H)))

    x2 = x.reshape(T, D)
    u = _norm_mod(x2, norm1_g, shift1, scale1, S)
    proj, ab, _ = _matmul(u, w_bf, BF16, n_cols=c_ab, name="inproj", narrow=w_ab)
    G, Cg, _ = pool_w.shape
    gates, _, (w_up_pool, w_up_delta, w_out, w_ff_in, pool_w) = _matmul(
        u, w_gates, BF16, name="inproj_gates",
        cast_along=(w_up_pool, w_up_delta, w_out, w_ff_in, pool_w.reshape(G * Cg, Cg)))

    yp = _pool(proj, pool_w.reshape(G, Cg, Cg), pool_scale, S)
    yd = _delta(proj, ab, conv_w, a_log, dt_bias, head_norm_g,
                batch=B, seq=S, n_heads=H, col0=PW)
    merged, w_ff_out = _merge(yp, yd, w_up_pool, w_up_delta, gates, cast_along=w_ff_out)
    h = _outproj(merged, w_out, x2, gate1, S)

    u2 = _norm_mod(h, norm2_g, shift2, scale2, S)
    out = _ffn(u2, w_ff_in, w_ff_out, h, gate2, final_g, S)
    return out.reshape(B, S, D)
```

```python
import functools

import jax
import jax.numpy as jnp
from jax import lax
from jax.experimental import pallas as pl
from jax.experimental.pallas import tpu as pltpu

EPS = 1e-6
HEAD_DIM = 128
CHUNK = 64
CONV_WIDTH = 4
POOL_WINDOWS = (2, 4, 8, 16)
N_MOD = 6
LANES = 128
SUBLANES = 8
BF16_ROWS = 16
VMEM_LIMIT = 56 * 1024 * 1024
FFN_TM, FFN_TF = 1024, 512
FFN_EPILOGUE_ROWS = 128
FFN_VMEM_LIMIT = 60 * 1024 * 1024
F32 = jnp.float32
BF16 = jnp.bfloat16
HIGHEST = lax.Precision.HIGHEST


def _params(sem):
    return pltpu.CompilerParams(dimension_semantics=sem, vmem_limit_bytes=VMEM_LIMIT)


def _tile(n, pref):
    t = min(n, pref)
    while n % t:
        t //= 2
    return t


def _silu(x):
    return x * jax.nn.sigmoid(x)


def _ada_kernel(c_ref, w_ref, b_ref, o_ref):
    s = _silu(c_ref[...]).astype(BF16)
    o_ref[...] = jnp.dot(s, w_ref[...].astype(BF16), preferred_element_type=F32) + b_ref[...]


def _ada(c, w_ada, b_ada):
    B, D = c.shape
    N = w_ada.shape[1]
    rows = SUBLANES * pl.cdiv(B, SUBLANES)
    cp = jnp.zeros((rows, D), F32).at[:B].set(c)
    tn = _tile(N, 512)
    out = pl.pallas_call(
        _ada_kernel,
        grid=(N // tn,),
        in_specs=[pl.BlockSpec((rows, D), lambda j: (0, 0)),
                  pl.BlockSpec((D, tn), lambda j: (0, j)),
                  pl.BlockSpec((1, tn), lambda j: (0, j))],
        out_specs=pl.BlockSpec((rows, tn), lambda j: (0, j)),
        out_shape=jax.ShapeDtypeStruct((rows, N), F32),
        compiler_params=_params(("parallel",)),
        name="ada",
    )(cp, w_ada, b_ada.reshape(1, N))
    return out[:B]


def _norm_kernel(x_ref, g_ref, shift_ref, scale_ref, o_ref):
    x = x_ref[...]
    n = x * lax.rsqrt(jnp.mean(x * x, axis=-1, keepdims=True) + EPS) * g_ref[...]
    o_ref[...] = (n * (1.0 + scale_ref[0]) + shift_ref[0]).astype(o_ref.dtype)


def _norm_mod(x2, g, shift, scale, seq):
    T, D = x2.shape
    tm = _tile(seq, 512)
    bmap = lambda i: ((i * tm) // seq, 0, 0)
    return pl.pallas_call(
        _norm_kernel,
        grid=(T // tm,),
        in_specs=[pl.BlockSpec((tm, D), lambda i: (i, 0)),
                  pl.BlockSpec((1, D), lambda i: (0, 0)),
                  pl.BlockSpec((1, 1, D), bmap),
                  pl.BlockSpec((1, 1, D), bmap)],
        out_specs=pl.BlockSpec((tm, D), lambda i: (i, 0)),
        out_shape=jax.ShapeDtypeStruct((T, D), BF16),
        compiler_params=_params(("parallel",)),
        name="norm_mod",
    )(x2, g.reshape(1, D), shift, scale)


def _mm_kernel(a_ref, w_ref, *refs, n_narrow):
    n_side = (len(refs) - 1 - 2 * n_narrow) // 2
    ins, outs = refs[:n_narrow + n_side], refs[n_narrow + n_side:]
    o_ref = outs[0]
    o_ref[...] = jnp.dot(a_ref[...], w_ref[...], preferred_element_type=F32).astype(o_ref.dtype)
    if n_narrow:
        @pl.when(pl.program_id(1) == 0)
        def _():
            outs[1][...] = jnp.dot(a_ref[...], ins[0][...], preferred_element_type=F32)
    for src, dst in zip(ins[n_narrow:], outs[1 + n_narrow:]):
        dst[...] = src[...].astype(dst.dtype)


def _matmul(a, w, out_dtype, n_cols=None, tm_pref=1024, tn_pref=1024, name="mm", narrow=None, cast_along=()):
    M, K = a.shape
    N = w.shape[1] if n_cols is None else n_cols
    tm, tn = _tile(M, tm_pref), _tile(N, tn_pref)
    ni, nj = M // tm, N // tn
    steps = ni * nj
    side_specs = []
    for t in cast_along:
        rows = t.shape[0] // steps
        assert rows * steps == t.shape[0] and rows % BF16_ROWS == 0, (t.shape, steps)
        side_specs.append(pl.BlockSpec((rows, t.shape[1]), lambda i, j: (i * nj + j, 0)))
    nar_in = [] if narrow is None else [pl.BlockSpec(narrow.shape, lambda i, j: (0, 0))]
    nar_out = [] if narrow is None else [pl.BlockSpec((tm, narrow.shape[1]), lambda i, j: (i, 0))]
    nar_shape = [] if narrow is None else [jax.ShapeDtypeStruct((M, narrow.shape[1]), F32)]
    outs = pl.pallas_call(
        functools.partial(_mm_kernel, n_narrow=len(nar_in)),
        grid=(ni, nj),
        in_specs=[pl.BlockSpec((tm, K), lambda i, j: (i, 0)),
                  pl.BlockSpec((K, tn), lambda i, j: (0, j))] + nar_in + side_specs,
        out_specs=[pl.BlockSpec((tm, tn), lambda i, j: (i, j))] + nar_out + side_specs,
        out_shape=[jax.ShapeDtypeStruct((M, N), out_dtype)] + nar_shape
                  + [jax.ShapeDtypeStruct(t.shape, BF16) for t in cast_along],
        compiler_params=_params(("arbitrary", "arbitrary")),
        name=name,
    )(a, w, *([] if narrow is None else [narrow]), *cast_along)
    return outs[0], (None if narrow is None else outs[1]), outs[1 + len(nar_in):]


def _pool_kernel(p_ref, halo_ref, w_ref, scale_ref, o_ref, *, seq, tm, halo):
    i = pl.program_id(0)
    g = pl.program_id(1)
    start = (i * tm) % seq
    p = p_ref[...]
    hl = jnp.where(start > 0, halo_ref[...], jnp.zeros_like(halo_ref))
    ext = jnp.concatenate([jnp.zeros((LANES - halo, p.shape[1]), p.dtype), hl, p], axis=0)
    win = jnp.where(g == 0, POOL_WINDOWS[0],
                    jnp.where(g == 1, POOL_WINDOWS[1],
                              jnp.where(g == 2, POOL_WINDOWS[2], POOL_WINDOWS[3])))
    r = lax.broadcasted_iota(jnp.int32, (tm, tm + LANES), 0)
    cidx = lax.broadcasted_iota(jnp.int32, (tm, tm + LANES), 1)
    d = r + LANES - cidx
    band = jnp.where((d >= 0) & (d < win), 1.0, 0.0).astype(BF16)
    wsum = jnp.dot(band, ext, preferred_element_type=F32)
    pos = start + lax.broadcasted_iota(jnp.int32, (tm, 1), 0) + 1
    count = jnp.minimum(pos, win).astype(F32)
    mixed = wsum / count - p.astype(F32)
    y = jnp.dot(mixed.astype(BF16), w_ref[0], preferred_element_type=F32)
    o_ref[...] = (y * scale_ref[...]).astype(o_ref.dtype)


def _pool(proj, pool_w_bf, pool_scale, seq):
    T = proj.shape[0]
    G, Cg, _ = pool_w_bf.shape
    assert G == len(POOL_WINDOWS)
    halo = BF16_ROWS
    tm = _tile(seq, 512)
    hb = tm // halo
    kern = functools.partial(_pool_kernel, seq=seq, tm=tm, halo=halo)
    return pl.pallas_call(
        kern,
        grid=(T // tm, G),
        in_specs=[pl.BlockSpec((tm, Cg), lambda i, g: (i, g)),
                  pl.BlockSpec((halo, Cg), lambda i, g: (jnp.maximum(i * hb - 1, 0), g)),
                  pl.BlockSpec((1, Cg, Cg), lambda i, g: (g, 0, 0)),
                  pl.BlockSpec((1, Cg), lambda i, g: (0, g))],
        out_specs=pl.BlockSpec((tm, Cg), lambda i, g: (i, g)),
        out_shape=jax.ShapeDtypeStruct((T, G * Cg), BF16),
        compiler_params=_params(("parallel", "parallel")),
        name="pool",
    )(proj, proj, pool_w_bf, pool_scale.reshape(1, G * Cg))


SOLVE_BASE_LOG2 = 3
CHUNKS_PER_ITER = 2


def _bdot(a, b):
    return jnp.dot(a.astype(BF16), b.astype(BF16), preferred_element_type=F32)


def _solve_unit_lower(L, X, ri, ci):
    b = SOLVE_BASE_LOG2
    A = jnp.where((ri >> b) == (ci >> b), L, 0.0)
    A2 = _bdot(A, A)
    yield
    A3 = _bdot(A, A2)
    A4 = _bdot(A2, A2)
    yield
    T0 = A2 - A - A3
    Tm = T0 + A4 + _bdot(T0, A4)
    yield
    lvl = b
    while (1 << lvl) < CHUNK:
        Cm = jnp.where(((ri >> (lvl + 1)) == (ci >> (lvl + 1))) & ((ri >> lvl) != (ci >> lvl)), L, 0.0)
        M = Cm + _bdot(Cm, Tm)
        yield
        Tm = Tm - M - _bdot(Tm, M)
        yield
        lvl += 1
    return X + _bdot(Tm, X)


def _round_robin(gens, on_done):
    live = list(range(len(gens)))
    while live:
        for n in list(live):
            try:
                next(gens[n])
            except StopIteration as e:
                on_done(n, e.value)
                live.remove(n)


def _delta_kernel(q_ref, k_ref, v_ref, z_ref, ab_ref, wq_ref, wk_ref, wv_ref, nega_ref, dtb_ref,
                  hng_ref, o_ref, s_scr, halo_scr, qs_scr, ks_scr, vs_scr, g_scr, gt_scr, beta_scr,
                  *, hb, rows, n_heads):
    hg = pl.program_id(1)
    cb = pl.program_id(2)

    @pl.when(cb == 0)
    def _():
        s_scr[...] = jnp.zeros_like(s_scr)
        halo_scr[...] = jnp.zeros_like(halo_scr)

    def conv(idx, x_ref, w_ref, dst, mult):
        for j in range(hb):
            cs = slice(j * HEAD_DIM, (j + 1) * HEAD_DIM)
            w = w_ref[:, cs]
            for r0 in range(0, rows, CHUNK):
                x = x_ref[r0:r0 + CHUNK, cs].astype(F32)
                prev = (halo_scr[idx, :, cs] if r0 == 0
                        else x_ref[r0 - BF16_ROWS:r0, cs].astype(F32)[-SUBLANES:])
                ext = jnp.concatenate([prev, x], axis=0)
                y = x * w[CONV_WIDTH - 1:CONV_WIDTH]
                for s in range(1, CONV_WIDTH):
                    y = y + ext[SUBLANES - s:SUBLANES - s + CHUNK] * w[CONV_WIDTH - 1 - s:CONV_WIDTH - s]
                y = _silu(y)
                if mult is not None:
                    y = y * (lax.rsqrt(jnp.sum(y * y, axis=-1, keepdims=True) + EPS) * mult)
                dst[r0:r0 + CHUNK, cs] = y
        halo_scr[idx] = x_ref[rows - BF16_ROWS:rows, :].astype(F32)[-SUBLANES:]

    conv(0, q_ref, wq_ref, qs_scr, HEAD_DIM ** -0.5)
    conv(1, k_ref, wk_ref, ks_scr, 1.0)
    conv(2, v_ref, wv_ref, vs_scr, None)

    ri = lax.broadcasted_iota(jnp.int32, (CHUNK, CHUNK), 0)
    ci = lax.broadcasted_iota(jnp.int32, (CHUNK, CHUNK), 1)
    causal = ri >= ci
    strict = ri > ci
    tri = causal.astype(F32)

    ab = ab_ref[...]
    ld = nega_ref[...] * jax.nn.softplus(ab + dtb_ref[...])
    beta_scr[...] = jax.nn.sigmoid(ab)
    for cc in range(rows // CHUNK):
        g = jnp.dot(tri, ld[cc * CHUNK:(cc + 1) * CHUNK], preferred_element_type=F32, precision=HIGHEST)
        g_scr[cc * CHUNK:(cc + 1) * CHUNK, :] = g
        gt_scr[cc] = g.T

    lane = lax.broadcasted_iota(jnp.int32, (CHUNK, LANES), 1)

    hng = hng_ref[...]

    def chunk_head(h, q, k, v, z, get_state, gch, g_row, btc):
        gb = jnp.broadcast_to(jnp.sum(jnp.where(lane == h, gch, 0.0), axis=-1, keepdims=True),
                              (CHUNK, LANES))
        beta_b = jnp.sum(jnp.where(lane == n_heads + h, btc, 0.0), axis=-1, keepdims=True)
        diff = gb[:, :CHUNK] - g_row
        decay = jnp.where(causal, jnp.exp(jnp.where(causal, diff, 0.0)), 0.0)
        eg = jnp.exp(gb)
        g_last = gb[CHUNK - 1:CHUNK, :]
        ew = jnp.exp(g_last - gb)
        e_last = jnp.exp(g_last)

        kb = k * beta_b
        kq = lax.dot_general(jnp.concatenate([kb, q], axis=0).astype(BF16), k.astype(BF16),
                             (((1,), (1,)), ((), ())), preferred_element_type=F32)
        yield
        L = jnp.where(strict, kq[:CHUNK] * decay, 0.0)
        intra = kq[CHUNK:] * decay
        X = jnp.concatenate([v * beta_b, kb * eg], axis=1)
        X = yield from _solve_unit_lower(L, X, ri, ci)
        yield
        Xb = X.astype(BF16)
        iv = jnp.dot(intra.astype(BF16), Xb, preferred_element_type=F32)
        kv = lax.dot_general((k * ew).astype(BF16), Xb, (((0,), (0,)), ((), ())),
                             preferred_element_type=F32)
        yield
        qe = q * eg - iv[:, HEAD_DIM:]
        lhs = jnp.concatenate([kv[:, HEAD_DIM:], qe], axis=0).astype(BF16)
        S = get_state()
        rs = jnp.dot(lhs, S.astype(BF16), preferred_element_type=F32)
        s_new = e_last * S - rs[:HEAD_DIM] + kv[:, :HEAD_DIM]
        o = rs[HEAD_DIM:] + iv[:, :HEAD_DIM]
        o = o * lax.rsqrt(jnp.mean(o * o, axis=-1, keepdims=True) + EPS) * hng
        return s_new, (o * _silu(z)).astype(o_ref.dtype)

    def chunk_body(cc, carry):
        gens, state = [], {}
        for u in range(CHUNKS_PER_ITER):
            c = cc * CHUNKS_PER_ITER + u
            rsl = pl.ds(pl.multiple_of(c * CHUNK, CHUNK), CHUNK)
            gch = g_scr[rsl, :]
            btc = beta_scr[rsl, :]
            for j in range(hb):
                sl = slice(j * HEAD_DIM, (j + 1) * HEAD_DIM)
                if u == 0:
                    state[j] = s_scr[j]
                gens.append(chunk_head(hg * hb + j, qs_scr[rsl, sl], ks_scr[rsl, sl], vs_scr[rsl, sl],
                                       z_ref[rsl, sl].astype(F32), functools.partial(state.get, j),
                                       gch, gt_scr[c, pl.ds(hg * hb + j, 1), :], btc))
        outs = [None] * len(gens)

        def done(n, value):
            outs[n] = value[1]
            state[n % hb] = value[0]

        _round_robin(gens, done)
        for u in range(CHUNKS_PER_ITER):
            rsl = pl.ds(pl.multiple_of((cc * CHUNKS_PER_ITER + u) * CHUNK, CHUNK), CHUNK)
            for j in range(hb):
                o_ref[rsl, j * HEAD_DIM:(j + 1) * HEAD_DIM] = outs[u * hb + j]
        for j in range(hb):
            s_scr[j] = state[j]
        return carry

    lax.fori_loop(0, rows // (CHUNK * CHUNKS_PER_ITER), chunk_body, 0)


def _delta(proj, ab, conv_w, a_log, dt_bias, head_norm_g, *, batch, seq, n_heads, col0):
    T = proj.shape[0]
    KW = n_heads * HEAD_DIM
    hb = next(d for d in (16, 8, 4, 2, 1) if n_heads % d == 0)
    C = hb * HEAD_DIM
    rows = _tile(seq, 256)
    nb = seq // rows
    assert col0 % C == 0 and KW % C == 0 and n_heads <= LANES // 2 and rows % (CHUNK * CHUNKS_PER_ITER) == 0
    qb, kb_, vb, zb = ((col0 + s * KW) // C for s in range(4))
    row = lambda b, g, c: b * nb + c
    pad = lambda t: jnp.zeros((1, LANES), F32).at[0, :n_heads].set(t)
    nega = pad(-jnp.exp(a_log.astype(F32)))
    dtb = pad(dt_bias.astype(F32))
    wq, wk, wv = (conv_w[:, s * KW:(s + 1) * KW] for s in range(3))
    kern = functools.partial(_delta_kernel, hb=hb, rows=rows, n_heads=n_heads)
    wspec = pl.BlockSpec((CONV_WIDTH, C), lambda b, g, c: (0, g))
    vec = pl.BlockSpec((1, LANES), lambda b, g, c: (0, 0))
    return pl.pallas_call(
        kern,
        grid=(batch, n_heads // hb, nb),
        in_specs=[pl.BlockSpec((rows, C), lambda b, g, c: (row(b, g, c), qb + g)),
                  pl.BlockSpec((rows, C), lambda b, g, c: (row(b, g, c), kb_ + g)),
                  pl.BlockSpec((rows, C), lambda b, g, c: (row(b, g, c), vb + g)),
                  pl.BlockSpec((rows, C), lambda b, g, c: (row(b, g, c), zb + g)),
                  pl.BlockSpec((rows, LANES), lambda b, g, c: (row(b, g, c), 0)),
                  wspec, wspec, wspec, vec, vec, vec],
        out_specs=pl.BlockSpec((rows, C), lambda b, g, c: (row(b, g, c), g)),
        out_shape=jax.ShapeDtypeStruct((T, KW), BF16),
        scratch_shapes=[pltpu.VMEM((hb, HEAD_DIM, HEAD_DIM), F32),
                        pltpu.VMEM((3, SUBLANES, C), F32),
                        pltpu.VMEM((rows, C), F32),
                        pltpu.VMEM((rows, C), F32),
                        pltpu.VMEM((rows, C), F32),
                        pltpu.VMEM((rows, LANES), F32),
                        pltpu.VMEM((rows // CHUNK, LANES, CHUNK), F32),
                        pltpu.VMEM((rows, LANES), F32)],
        compiler_params=_params(("parallel", "parallel", "arbitrary")),
        name="delta",
    )(proj, proj, proj, proj, ab, wq, wk, wv, nega, dtb, head_norm_g.reshape(1, HEAD_DIM).astype(F32))


def _merge_kernel(a1_ref, a2_ref, w1_ref, w2_ref, g1_ref, g2_ref, src_ref, o_ref, dst_ref):
    y1 = jnp.dot(a1_ref[...], w1_ref[...], preferred_element_type=F32)
    y2 = jnp.dot(a2_ref[...], w2_ref[...], preferred_element_type=F32)
    m = jax.nn.sigmoid(g1_ref[...].astype(F32)) * y1 + jax.nn.sigmoid(g2_ref[...].astype(F32)) * y2
    o_ref[...] = m.astype(o_ref.dtype)
    dst_ref[...] = src_ref[...].astype(dst_ref.dtype)


def _merge(yp, yd, w1, w2, gates, cast_along):
    T, K1 = yp.shape
    K2 = yd.shape[1]
    D = w1.shape[1]
    tm, tn = _tile(T, 1024), _tile(D, 512)
    ni, nj = T // tm, D // tn
    rows = cast_along.shape[0] // (ni * nj)
    assert rows * ni * nj == cast_along.shape[0] and rows % BF16_ROWS == 0
    side = pl.BlockSpec((rows, cast_along.shape[1]), lambda i, j: (i * nj + j, 0))
    return pl.pallas_call(
        _merge_kernel,
        grid=(ni, nj),
        in_specs=[pl.BlockSpec((tm, K1), lambda i, j: (i, 0)),
                  pl.BlockSpec((tm, K2), lambda i, j: (i, 0)),
                  pl.BlockSpec((K1, tn), lambda i, j: (0, j)),
                  pl.BlockSpec((K2, tn), lambda i, j: (0, j)),
                  pl.BlockSpec((tm, tn), lambda i, j: (i, j)),
                  pl.BlockSpec((tm, tn), lambda i, j: (i, nj + j)),
                  side],
        out_specs=[pl.BlockSpec((tm, tn), lambda i, j: (i, j)), side],
        out_shape=[jax.ShapeDtypeStruct((T, D), BF16), jax.ShapeDtypeStruct(cast_along.shape, BF16)],
        compiler_params=_params(("arbitrary", "arbitrary")),
        name="merge",
    )(yp, yd, w1, w2, gates, gates, cast_along)


def _outproj_kernel(a_ref, w_ref, x_ref, gate_ref, o_ref):
    acc = jnp.dot(a_ref[...], w_ref[...], preferred_element_type=F32)
    o_ref[...] = x_ref[...] + gate_ref[0] * acc


def _outproj(a, w, x2, gate, seq):
    T, K = a.shape
    D = w.shape[1]
    tm, tn = _tile(seq, 1024), _tile(D, 1024)
    return pl.pallas_call(
        _outproj_kernel,
        grid=(T // tm, D // tn),
        in_specs=[pl.BlockSpec((tm, K), lambda i, j: (i, 0)),
                  pl.BlockSpec((K, tn), lambda i, j: (0, j)),
                  pl.BlockSpec((tm, tn), lambda i, j: (i, j)),
                  pl.BlockSpec((1, 1, tn), lambda i, j: ((i * tm) // seq, 0, j))],
        out_specs=pl.BlockSpec((tm, tn), lambda i, j: (i, j)),
        out_shape=jax.ShapeDtypeStruct((T, D), F32),
        compiler_params=_params(("parallel", "parallel")),
        name="outproj",
    )(a, w, x2, gate)


def _ffn_kernel(u_ref, w1_ref, w2_ref, h_hbm, gate_ref, fg_ref, o_hbm, acc, hbuf, hsem, osem, *, nf, tn):
    i = pl.program_id(0)
    f = pl.program_id(1)
    tm = acc.shape[0]
    rows = hbuf.shape[1]
    nchunk = tm // rows

    def h_copy(r, slot):
        return pltpu.make_async_copy(h_hbm.at[pl.ds(i * tm + r * rows, rows), :], hbuf.at[slot], hsem.at[slot])

    def o_copy(r):
        return pltpu.make_async_copy(acc.at[pl.ds(r * rows, rows), :],
                                     o_hbm.at[pl.ds(i * tm + r * rows, rows), :], osem.at[r])

    @pl.when(f == 0)
    def _():
        acc[...] = jnp.zeros_like(acc)

    @pl.when(f == nf - 1)
    def _():
        h_copy(0, 0).start()

    hdn = jnp.maximum(jnp.dot(u_ref[...], w1_ref[...], preferred_element_type=F32), 0.0)
    hb = (hdn * hdn).astype(BF16)
    for n0 in range(0, acc.shape[1], tn):
        acc[:, n0:n0 + tn] += jnp.dot(hb, w2_ref[:, n0:n0 + tn], preferred_element_type=F32)

    @pl.when(f == nf - 1)
    def _():
        def body(r, carry):
            slot = lax.rem(r, 2)

            @pl.when(r + 1 < nchunk)
            def _():
                h_copy(r + 1, 1 - slot).start()

            h_copy(r, slot).wait()
            sl = pl.ds(pl.multiple_of(r * rows, rows), rows)
            h2 = hbuf[slot] + gate_ref[0] * acc[sl, :]
            acc[sl, :] = h2 * lax.rsqrt(jnp.mean(h2 * h2, axis=-1, keepdims=True) + EPS) * fg_ref[...]
            o_copy(r).start()
            return carry

        lax.fori_loop(0, nchunk, body, 0)
        for r in range(nchunk):
            o_copy(r).wait()


def _ffn(u2, w1, w2, h, gate, final_g, seq):
    T, D = u2.shape
    F = w1.shape[1]
    tm, tf = _tile(seq, FFN_TM), _tile(F, FFN_TF)
    nf = F // tf
    rows = _tile(tm, FFN_EPILOGUE_ROWS)
    return pl.pallas_call(
        functools.partial(_ffn_kernel, nf=nf, tn=_tile(D, 1024)),
        grid=(T // tm, nf),
        in_specs=[pl.BlockSpec((tm, D), lambda i, f: (i, 0)),
                  pl.BlockSpec((D, tf), lambda i, f: (0, f)),
                  pl.BlockSpec((tf, D), lambda i, f: (f, 0)),
                  pl.BlockSpec(memory_space=pl.ANY),
                  pl.BlockSpec((1, 1, D), lambda i, f: ((i * tm) // seq, 0, 0)),
                  pl.BlockSpec((1, D), lambda i, f: (0, 0))],
        out_specs=pl.BlockSpec(memory_space=pl.ANY),
        out_shape=jax.ShapeDtypeStruct((T, D), F32),
        scratch_shapes=[pltpu.VMEM((tm, D), F32),
                        pltpu.VMEM((2, rows, D), F32),
                        pltpu.SemaphoreType.DMA((2,)),
                        pltpu.SemaphoreType.DMA((tm // rows,))],
        compiler_params=pltpu.CompilerParams(dimension_semantics=("arbitrary", "arbitrary"),
                                             vmem_limit_bytes=FFN_VMEM_LIMIT),
        name="ffn",
    )(u2, w1, w2, h, gate, final_g.reshape(1, D))


def kernel(x, c, w_ada, b_ada, norm1_g, w_in, conv_w, pool_w, pool_scale, a_log, dt_bias,
           head_norm_g, w_up_pool, w_up_delta, w_out, norm2_g, w_ff_in, w_ff_out, final_g):
    B, S, D = x.shape
    T = B * S
    H = a_log.shape[0]
    KW = H * HEAD_DIM
    PW = pool_scale.shape[0]
    assert w_in.shape[1] == PW + 3 * KW + KW + 2 * H + 2 * D
    assert w_ada.shape[1] == N_MOD * D and S % CHUNK == 0 and 2 * H <= LANES

    mod = _ada(c, w_ada, b_ada)
    shift1, scale1, gate1, shift2, scale2, gate2 = (
        mod[:, i * D:(i + 1) * D].reshape(B, 1, D) for i in range(N_MOD))

    c_ab = PW + 4 * KW
    w_bf = w_in.astype(BF16)
    w_gates = w_bf[:, c_ab + 2 * H:]
    w_ab = jnp.pad(w_bf[:, c_ab:c_ab + 2 * H], ((0, 0), (0, LANES - 2 * H)))

    x2 = x.reshape(T, D)
    u = _norm_mod(x2, norm1_g, shift1, scale1, S)
    proj, ab, _ = _matmul(u, w_bf, BF16, n_cols=c_ab, name="inproj", narrow=w_ab)
    G, Cg, _ = pool_w.shape
    gates, _, (w_up_pool, w_up_delta, w_out, w_ff_in, pool_w) = _matmul(
        u, w_gates, BF16, name="inproj_gates",
        cast_along=(w_up_pool, w_up_delta, w_out, w_ff_in, pool_w.reshape(G * Cg, Cg)))

    yp = _pool(proj, pool_w.reshape(G, Cg, Cg), pool_scale, S)
    yd = _delta(proj, ab, conv_w, a_log, dt_bias, head_norm_g,
                batch=B, seq=S, n_heads=H, col0=PW)
    merged, w_ff_out = _merge(yp, yd, w_up_pool, w_up_delta, gates, cast_along=w_ff_out)
    h = _outproj(merged, w_out, x2, gate1, S)

    u2 = _norm_mod(h, norm2_g, shift2, scale2, S)
    out = _ffn(u2, w_ff_in, w_ff_out, h, gate2, final_g, S)
    return out.reshape(B, S, D)
```

```python
import functools

import jax
import jax.numpy as jnp
from jax import lax
from jax.experimental import pallas as pl
from jax.experimental.pallas import tpu as pltpu

EPS = 1e-6
HEAD_DIM = 128
CHUNK = 64
CONV_WIDTH = 4
POOL_WINDOWS = (2, 4, 8, 16)
N_MOD = 6
LANES = 128
SUBLANES = 8
BF16_ROWS = 16
VMEM_LIMIT = 56 * 1024 * 1024
FFN_TM, FFN_TF = 1024, 512
FFN_EPILOGUE_ROWS = 128
FFN_VMEM_LIMIT = 60 * 1024 * 1024
F32 = jnp.float32
BF16 = jnp.bfloat16
HIGHEST = lax.Precision.HIGHEST


def _params(sem):
    return pltpu.CompilerParams(dimension_semantics=sem, vmem_limit_bytes=VMEM_LIMIT)


def _tile(n, pref):
    t = min(n, pref)
    while n % t:
        t //= 2
    return t


def _silu(x):
    return x * jax.nn.sigmoid(x)


def _ada_kernel(c_ref, w_ref, b_ref, o_ref):
    s = _silu(c_ref[...]).astype(BF16)
    o_ref[...] = jnp.dot(s, w_ref[...].astype(BF16), preferred_element_type=F32) + b_ref[...]


def _ada(c, w_ada, b_ada):
    B, D = c.shape
    N = w_ada.shape[1]
    rows = SUBLANES * pl.cdiv(B, SUBLANES)
    cp = jnp.zeros((rows, D), F32).at[:B].set(c)
    tn = _tile(N, 512)
    out = pl.pallas_call(
        _ada_kernel,
        grid=(N // tn,),
        in_specs=[pl.BlockSpec((rows, D), lambda j: (0, 0)),
                  pl.BlockSpec((D, tn), lambda j: (0, j)),
                  pl.BlockSpec((1, tn), lambda j: (0, j))],
        out_specs=pl.BlockSpec((rows, tn), lambda j: (0, j)),
        out_shape=jax.ShapeDtypeStruct((rows, N), F32),
        compiler_params=_params(("parallel",)),
        name="ada",
    )(cp, w_ada, b_ada.reshape(1, N))
    return out[:B]


def _norm_kernel(x_ref, g_ref, shift_ref, scale_ref, o_ref):
    x = x_ref[...]
    n = x * lax.rsqrt(jnp.mean(x * x, axis=-1, keepdims=True) + EPS) * g_ref[...]
    o_ref[...] = (n * (1.0 + scale_ref[0]) + shift_ref[0]).astype(o_ref.dtype)


def _norm_mod(x2, g, shift, scale, seq):
    T, D = x2.shape
    tm = _tile(seq, 512)
    bmap = lambda i: ((i * tm) // seq, 0, 0)
    return pl.pallas_call(
        _norm_kernel,
        grid=(T // tm,),
        in_specs=[pl.BlockSpec((tm, D), lambda i: (i, 0)),
                  pl.BlockSpec((1, D), lambda i: (0, 0)),
                  pl.BlockSpec((1, 1, D), bmap),
                  pl.BlockSpec((1, 1, D), bmap)],
        out_specs=pl.BlockSpec((tm, D), lambda i: (i, 0)),
        out_shape=jax.ShapeDtypeStruct((T, D), BF16),
        compiler_params=_params(("parallel",)),
        name="norm_mod",
    )(x2, g.reshape(1, D), shift, scale)


def _mm_kernel(a_ref, w_ref, *refs, n_narrow):
    n_side = (len(refs) - 1 - 2 * n_narrow) // 2
    ins, outs = refs[:n_narrow + n_side], refs[n_narrow + n_side:]
    o_ref = outs[0]
    o_ref[...] = jnp.dot(a_ref[...], w_ref[...], preferred_element_type=F32).astype(o_ref.dtype)
    if n_narrow:
        @pl.when(pl.program_id(1) == 0)
        def _():
            outs[1][...] = jnp.dot(a_ref[...], ins[0][...], preferred_element_type=F32)
    for src, dst in zip(ins[n_narrow:], outs[1 + n_narrow:]):
        dst[...] = src[...].astype(dst.dtype)


def _matmul(a, w, out_dtype, n_cols=None, tm_pref=1024, tn_pref=1024, name="mm", narrow=None, cast_along=()):
    M, K = a.shape
    N = w.shape[1] if n_cols is None else n_cols
    tm, tn = _tile(M, tm_pref), _tile(N, tn_pref)
    ni, nj = M // tm, N // tn
    steps = ni * nj
    side_specs = []
    for t in cast_along:
        rows = t.shape[0] // steps
        assert rows * steps == t.shape[0] and rows % BF16_ROWS == 0, (t.shape, steps)
        side_specs.append(pl.BlockSpec((rows, t.shape[1]), lambda i, j: (i * nj + j, 0)))
    nar_in = [] if narrow is None else [pl.BlockSpec(narrow.shape, lambda i, j: (0, 0))]
    nar_out = [] if narrow is None else [pl.BlockSpec((tm, narrow.shape[1]), lambda i, j: (i, 0))]
    nar_shape = [] if narrow is None else [jax.ShapeDtypeStruct((M, narrow.shape[1]), F32)]
    outs = pl.pallas_call(
        functools.partial(_mm_kernel, n_narrow=len(nar_in)),
        grid=(ni, nj),
        in_specs=[pl.BlockSpec((tm, K), lambda i, j: (i, 0)),
                  pl.BlockSpec((K, tn), lambda i, j: (0, j))] + nar_in + side_specs,
        out_specs=[pl.BlockSpec((tm, tn), lambda i, j: (i, j))] + nar_out + side_specs,
        out_shape=[jax.ShapeDtypeStruct((M, N), out_dtype)] + nar_shape
                  + [jax.ShapeDtypeStruct(t.shape, BF16) for t in cast_along],
        compiler_params=_params(("arbitrary", "arbitrary")),
        name=name,
    )(a, w, *([] if narrow is None else [narrow]), *cast_along)
    return outs[0], (None if narrow is None else outs[1]), outs[1 + len(nar_in):]


def _pool_kernel(p_ref, halo_ref, w_ref, scale_ref, o_ref, *, seq, tm, halo):
    i = pl.program_id(0)
    g = pl.program_id(1)
    start = (i * tm) % seq
    p = p_ref[...]
    hl = jnp.where(start > 0, halo_ref[...], jnp.zeros_like(halo_ref))
    ext = jnp.concatenate([jnp.zeros((LANES - halo, p.shape[1]), p.dtype), hl, p], axis=0)
    win = jnp.where(g == 0, POOL_WINDOWS[0],
                    jnp.where(g == 1, POOL_WINDOWS[1],
                              jnp.where(g == 2, POOL_WINDOWS[2], POOL_WINDOWS[3])))
    r = lax.broadcasted_iota(jnp.int32, (LANES, 2 * LANES), 0)
    cidx = lax.broadcasted_iota(jnp.int32, (LANES, 2 * LANES), 1)
    d = r + LANES - cidx
    band = jnp.where((d >= 0) & (d < win), 1.0, 0.0).astype(BF16)
    wsum = jnp.concatenate(
        [jnp.dot(band, ext[b:b + 2 * LANES], preferred_element_type=F32) for b in range(0, tm, LANES)],
        axis=0)
    pos = start + lax.broadcasted_iota(jnp.int32, (tm, 1), 0) + 1
    count = jnp.minimum(pos, win).astype(F32)
    mixed = wsum / count - p.astype(F32)
    y = jnp.dot(mixed.astype(BF16), w_ref[0], preferred_element_type=F32)
    o_ref[...] = (y * scale_ref[...]).astype(o_ref.dtype)


def _pool(proj, pool_w_bf, pool_scale, seq):
    T = proj.shape[0]
    G, Cg, _ = pool_w_bf.shape
    assert G == len(POOL_WINDOWS)
    halo = BF16_ROWS
    tm = _tile(seq, 512)
    hb = tm // halo
    kern = functools.partial(_pool_kernel, seq=seq, tm=tm, halo=halo)
    return pl.pallas_call(
        kern,
        grid=(T // tm, G),
        in_specs=[pl.BlockSpec((tm, Cg), lambda i, g: (i, g)),
                  pl.BlockSpec((halo, Cg), lambda i, g: (jnp.maximum(i * hb - 1, 0), g)),
                  pl.BlockSpec((1, Cg, Cg), lambda i, g: (g, 0, 0)),
                  pl.BlockSpec((1, Cg), lambda i, g: (0, g))],
        out_specs=pl.BlockSpec((tm, Cg), lambda i, g: (i, g)),
        out_shape=jax.ShapeDtypeStruct((T, G * Cg), BF16),
        compiler_params=_params(("parallel", "parallel")),
        name="pool",
    )(proj, proj, pool_w_bf, pool_scale.reshape(1, G * Cg))


SOLVE_BASE_LOG2 = 3
CHUNKS_PER_ITER = 2


def _bdot(a, b):
    return jnp.dot(a.astype(BF16), b.astype(BF16), preferred_element_type=F32)


def _solve_unit_lower(L, X, ri, ci):
    b = SOLVE_BASE_LOG2
    A = jnp.where((ri >> b) == (ci >> b), L, 0.0)
    A2 = _bdot(A, A)
    yield
    A3 = _bdot(A, A2)
    A4 = _bdot(A2, A2)
    yield
    T0 = A2 - A - A3
    Tm = T0 + A4 + _bdot(T0, A4)
    yield
    lvl = b
    while (1 << lvl) < CHUNK:
        Cm = jnp.where(((ri >> (lvl + 1)) == (ci >> (lvl + 1))) & ((ri >> lvl) != (ci >> lvl)), L, 0.0)
        M = Cm + _bdot(Cm, Tm)
        yield
        Tm = Tm - M - _bdot(Tm, M)
        yield
        lvl += 1
    return X + _bdot(Tm, X)


def _round_robin(gens, on_done):
    live = list(range(len(gens)))
    while live:
        for n in list(live):
            try:
                next(gens[n])
            except StopIteration as e:
                on_done(n, e.value)
                live.remove(n)


def _delta_kernel(q_ref, k_ref, v_ref, z_ref, ab_ref, wq_ref, wk_ref, wv_ref, nega_ref, dtb_ref,
                  hng_ref, o_ref, s_scr, halo_scr, qs_scr, ks_scr, vs_scr, g_scr, gt_scr, beta_scr,
                  *, hb, rows, n_heads):
    hg = pl.program_id(1)
    cb = pl.program_id(2)

    @pl.when(cb == 0)
    def _():
        s_scr[...] = jnp.zeros_like(s_scr)
        halo_scr[...] = jnp.zeros_like(halo_scr)

    def conv(idx, x_ref, w_ref, dst, mult):
        for j in range(hb):
            cs = slice(j * HEAD_DIM, (j + 1) * HEAD_DIM)
            w = w_ref[:, cs]
            for r0 in range(0, rows, CHUNK):
                x = x_ref[r0:r0 + CHUNK, cs].astype(F32)
                prev = (halo_scr[idx, :, cs] if r0 == 0
                        else x_ref[r0 - BF16_ROWS:r0, cs].astype(F32)[-SUBLANES:])
                ext = jnp.concatenate([prev, x], axis=0)
                y = x * w[CONV_WIDTH - 1:CONV_WIDTH]
                for s in range(1, CONV_WIDTH):
                    y = y + ext[SUBLANES - s:SUBLANES - s + CHUNK] * w[CONV_WIDTH - 1 - s:CONV_WIDTH - s]
                y = _silu(y)
                if mult is not None:
                    y = y * (lax.rsqrt(jnp.sum(y * y, axis=-1, keepdims=True) + EPS) * mult)
                dst[r0:r0 + CHUNK, cs] = y
        halo_scr[idx] = x_ref[rows - BF16_ROWS:rows, :].astype(F32)[-SUBLANES:]

    conv(0, q_ref, wq_ref, qs_scr, HEAD_DIM ** -0.5)
    conv(1, k_ref, wk_ref, ks_scr, 1.0)
    conv(2, v_ref, wv_ref, vs_scr, None)

    ri = lax.broadcasted_iota(jnp.int32, (CHUNK, CHUNK), 0)
    ci = lax.broadcasted_iota(jnp.int32, (CHUNK, CHUNK), 1)
    causal = ri >= ci
    strict = ri > ci
    tri = causal.astype(F32)

    ab = ab_ref[...]
    ld = nega_ref[...] * jax.nn.softplus(ab + dtb_ref[...])
    beta_scr[...] = jax.nn.sigmoid(ab)
    for cc in range(rows // CHUNK):
        g = jnp.dot(tri, ld[cc * CHUNK:(cc + 1) * CHUNK], preferred_element_type=F32, precision=HIGHEST)
        g_scr[cc * CHUNK:(cc + 1) * CHUNK, :] = g
        gt_scr[cc] = g.T

    lane = lax.broadcasted_iota(jnp.int32, (CHUNK, LANES), 1)

    hng = hng_ref[...]

    def chunk_head(h, q, k, v, z, get_state, gch, g_row, btc):
        gb = jnp.broadcast_to(jnp.sum(jnp.where(lane == h, gch, 0.0), axis=-1, keepdims=True),
                              (CHUNK, LANES))
        beta_b = jnp.sum(jnp.where(lane == n_heads + h, btc, 0.0), axis=-1, keepdims=True)
        diff = gb[:, :CHUNK] - g_row
        decay = jnp.where(causal, jnp.exp(jnp.where(causal, diff, 0.0)), 0.0)
        eg = jnp.exp(gb)
        g_last = gb[CHUNK - 1:CHUNK, :]
        ew = jnp.exp(g_last - gb)
        e_last = jnp.exp(g_last)

        kb = k * beta_b
        kq = lax.dot_general(jnp.concatenate([kb, q], axis=0).astype(BF16), k.astype(BF16),
                             (((1,), (1,)), ((), ())), preferred_element_type=F32)
        yield
        L = jnp.where(strict, kq[:CHUNK] * decay, 0.0)
        intra = kq[CHUNK:] * decay
        X = jnp.concatenate([v * beta_b, kb * eg], axis=1)
        X = yield from _solve_unit_lower(L, X, ri, ci)
        yield
        Xb = X.astype(BF16)
        iv = jnp.dot(intra.astype(BF16), Xb, preferred_element_type=F32)
        kv = lax.dot_general((k * ew).astype(BF16), Xb, (((0,), (0,)), ((), ())),
                             preferred_element_type=F32)
        yield
        qe = q * eg - iv[:, HEAD_DIM:]
        lhs = jnp.concatenate([kv[:, HEAD_DIM:], qe], axis=0).astype(BF16)
        S = get_state()
        rs = jnp.dot(lhs, S.astype(BF16), preferred_element_type=F32)
        s_new = e_last * S - rs[:HEAD_DIM] + kv[:, :HEAD_DIM]
        o = rs[HEAD_DIM:] + iv[:, :HEAD_DIM]
        o = o * lax.rsqrt(jnp.mean(o * o, axis=-1, keepdims=True) + EPS) * hng
        return s_new, (o * _silu(z)).astype(o_ref.dtype)

    def chunk_body(cc, carry):
        gens, state = [], {}
        for u in range(CHUNKS_PER_ITER):
            c = cc * CHUNKS_PER_ITER + u
            rsl = pl.ds(pl.multiple_of(c * CHUNK, CHUNK), CHUNK)
            gch = g_scr[rsl, :]
            btc = beta_scr[rsl, :]
            for j in range(hb):
                sl = slice(j * HEAD_DIM, (j + 1) * HEAD_DIM)
                if u == 0:
                    state[j] = s_scr[j]
                gens.append(chunk_head(hg * hb + j, qs_scr[rsl, sl], ks_scr[rsl, sl], vs_scr[rsl, sl],
                                       z_ref[rsl, sl].astype(F32), functools.partial(state.get, j),
                                       gch, gt_scr[c, pl.ds(hg * hb + j, 1), :], btc))
        outs = [None] * len(gens)

        def done(n, value):
            outs[n] = value[1]
            state[n % hb] = value[0]

        _round_robin(gens, done)
        for u in range(CHUNKS_PER_ITER):
            rsl = pl.ds(pl.multiple_of((cc * CHUNKS_PER_ITER + u) * CHUNK, CHUNK), CHUNK)
            for j in range(hb):
                o_ref[rsl, j * HEAD_DIM:(j + 1) * HEAD_DIM] = outs[u * hb + j]
        for j in range(hb):
            s_scr[j] = state[j]
        return carry

    lax.fori_loop(0, rows // (CHUNK * CHUNKS_PER_ITER), chunk_body, 0)


def _delta(proj, ab, conv_w, a_log, dt_bias, head_norm_g, *, batch, seq, n_heads, col0):
    T = proj.shape[0]
    KW = n_heads * HEAD_DIM
    hb = next(d for d in (16, 8, 4, 2, 1) if n_heads % d == 0)
    C = hb * HEAD_DIM
    rows = _tile(seq, 256)
    nb = seq // rows
    assert col0 % C == 0 and KW % C == 0 and n_heads <= LANES // 2 and rows % (CHUNK * CHUNKS_PER_ITER) == 0
    qb, kb_, vb, zb = ((col0 + s * KW) // C for s in range(4))
    row = lambda b, g, c: b * nb + c
    pad = lambda t: jnp.zeros((1, LANES), F32).at[0, :n_heads].set(t)
    nega = pad(-jnp.exp(a_log.astype(F32)))
    dtb = pad(dt_bias.astype(F32))
    wq, wk, wv = (conv_w[:, s * KW:(s + 1) * KW] for s in range(3))
    kern = functools.partial(_delta_kernel, hb=hb, rows=rows, n_heads=n_heads)
    wspec = pl.BlockSpec((CONV_WIDTH, C), lambda b, g, c: (0, g))
    vec = pl.BlockSpec((1, LANES), lambda b, g, c: (0, 0))
    return pl.pallas_call(
        kern,
        grid=(batch, n_heads // hb, nb),
        in_specs=[pl.BlockSpec((rows, C), lambda b, g, c: (row(b, g, c), qb + g)),
                  pl.BlockSpec((rows, C), lambda b, g, c: (row(b, g, c), kb_ + g)),
                  pl.BlockSpec((rows, C), lambda b, g, c: (row(b, g, c), vb + g)),
                  pl.BlockSpec((rows, C), lambda b, g, c: (row(b, g, c), zb + g)),
                  pl.BlockSpec((rows, LANES), lambda b, g, c: (row(b, g, c), 0)),
                  wspec, wspec, wspec, vec, vec, vec],
        out_specs=pl.BlockSpec((rows, C), lambda b, g, c: (row(b, g, c), g)),
        out_shape=jax.ShapeDtypeStruct((T, KW), BF16),
        scratch_shapes=[pltpu.VMEM((hb, HEAD_DIM, HEAD_DIM), F32),
                        pltpu.VMEM((3, SUBLANES, C), F32),
                        pltpu.VMEM((rows, C), F32),
                        pltpu.VMEM((rows, C), F32),
                        pltpu.VMEM((rows, C), F32),
                        pltpu.VMEM((rows, LANES), F32),
                        pltpu.VMEM((rows // CHUNK, LANES, CHUNK), F32),
                        pltpu.VMEM((rows, LANES), F32)],
        compiler_params=_params(("parallel", "parallel", "arbitrary")),
        name="delta",
    )(proj, proj, proj, proj, ab, wq, wk, wv, nega, dtb, head_norm_g.reshape(1, HEAD_DIM).astype(F32))


def _merge_kernel(a1_ref, a2_ref, w1_ref, w2_ref, g1_ref, g2_ref, src_ref, o_ref, dst_ref):
    y1 = jnp.dot(a1_ref[...], w1_ref[...], preferred_element_type=F32)
    y2 = jnp.dot(a2_ref[...], w2_ref[...], preferred_element_type=F32)
    m = jax.nn.sigmoid(g1_ref[...].astype(F32)) * y1 + jax.nn.sigmoid(g2_ref[...].astype(F32)) * y2
    o_ref[...] = m.astype(o_ref.dtype)
    dst_ref[...] = src_ref[...].astype(dst_ref.dtype)


def _merge(yp, yd, w1, w2, gates, cast_along):
    T, K1 = yp.shape
    K2 = yd.shape[1]
    D = w1.shape[1]
    tm, tn = _tile(T, 1024), _tile(D, 512)
    ni, nj = T // tm, D // tn
    rows = cast_along.shape[0] // (ni * nj)
    assert rows * ni * nj == cast_along.shape[0] and rows % BF16_ROWS == 0
    side = pl.BlockSpec((rows, cast_along.shape[1]), lambda i, j: (i * nj + j, 0))
    return pl.pallas_call(
        _merge_kernel,
        grid=(ni, nj),
        in_specs=[pl.BlockSpec((tm, K1), lambda i, j: (i, 0)),
                  pl.BlockSpec((tm, K2), lambda i, j: (i, 0)),
                  pl.BlockSpec((K1, tn), lambda i, j: (0, j)),
                  pl.BlockSpec((K2, tn), lambda i, j: (0, j)),
                  pl.BlockSpec((tm, tn), lambda i, j: (i, j)),
                  pl.BlockSpec((tm, tn), lambda i, j: (i, nj + j)),
                  side],
        out_specs=[pl.BlockSpec((tm, tn), lambda i, j: (i, j)), side],
        out_shape=[jax.ShapeDtypeStruct((T, D), BF16), jax.ShapeDtypeStruct(cast_along.shape, BF16)],
        compiler_params=_params(("arbitrary", "arbitrary")),
        name="merge",
    )(yp, yd, w1, w2, gates, gates, cast_along)


def _outproj_kernel(a_ref, w_ref, x_ref, gate_ref, o_ref):
    acc = jnp.dot(a_ref[...], w_ref[...], preferred_element_type=F32)
    o_ref[...] = x_ref[...] + gate_ref[0] * acc


def _outproj(a, w, x2, gate, seq):
    T, K = a.shape
    D = w.shape[1]
    tm, tn = _tile(seq, 1024), _tile(D, 1024)
    return pl.pallas_call(
        _outproj_kernel,
        grid=(T // tm, D // tn),
        in_specs=[pl.BlockSpec((tm, K), lambda i, j: (i, 0)),
                  pl.BlockSpec((K, tn), lambda i, j: (0, j)),
                  pl.BlockSpec((tm, tn), lambda i, j: (i, j)),
                  pl.BlockSpec((1, 1, tn), lambda i, j: ((i * tm) // seq, 0, j))],
        out_specs=pl.BlockSpec((tm, tn), lambda i, j: (i, j)),
        out_shape=jax.ShapeDtypeStruct((T, D), F32),
        compiler_params=_params(("parallel", "parallel")),
        name="outproj",
    )(a, w, x2, gate)


def _ffn_kernel(u_ref, w1_ref, w2_ref, h_hbm, gate_ref, fg_ref, o_hbm, acc, hbuf, hsem, osem, *, nf, tn):
    i = pl.program_id(0)
    f = pl.program_id(1)
    tm = acc.shape[0]
    rows = hbuf.shape[1]
    nchunk = tm // rows

    def h_copy(r, slot):
        return pltpu.make_async_copy(h_hbm.at[pl.ds(i * tm + r * rows, rows), :], hbuf.at[slot], hsem.at[slot])

    def o_copy(r):
        return pltpu.make_async_copy(acc.at[pl.ds(r * rows, rows), :],
                                     o_hbm.at[pl.ds(i * tm + r * rows, rows), :], osem.at[r])

    @pl.when(f == 0)
    def _():
        acc[...] = jnp.zeros_like(acc)

    @pl.when(f == nf - 1)
    def _():
        h_copy(0, 0).start()

    hdn = jnp.maximum(jnp.dot(u_ref[...], w1_ref[...], preferred_element_type=F32), 0.0)
    hb = (hdn * hdn).astype(BF16)
    for n0 in range(0, acc.shape[1], tn):
        acc[:, n0:n0 + tn] += jnp.dot(hb, w2_ref[:, n0:n0 + tn], preferred_element_type=F32)

    @pl.when(f == nf - 1)
    def _():
        def body(r, carry):
            slot = lax.rem(r, 2)

            @pl.when(r + 1 < nchunk)
            def _():
                h_copy(r + 1, 1 - slot).start()

            h_copy(r, slot).wait()
            sl = pl.ds(pl.multiple_of(r * rows, rows), rows)
            h2 = hbuf[slot] + gate_ref[0] * acc[sl, :]
            acc[sl, :] = h2 * lax.rsqrt(jnp.mean(h2 * h2, axis=-1, keepdims=True) + EPS) * fg_ref[...]
            o_copy(r).start()
            return carry

        lax.fori_loop(0, nchunk, body, 0)
        for r in range(nchunk):
            o_copy(r).wait()


def _ffn(u2, w1, w2, h, gate, final_g, seq):
    T, D = u2.shape
    F = w1.shape[1]
    tm, tf = _tile(seq, FFN_TM), _tile(F, FFN_TF)
    nf = F // tf
    rows = _tile(tm, FFN_EPILOGUE_ROWS)
    return pl.pallas_call(
        functools.partial(_ffn_kernel, nf=nf, tn=_tile(D, 1024)),
        grid=(T // tm, nf),
        in_specs=[pl.BlockSpec((tm, D), lambda i, f: (i, 0)),
                  pl.BlockSpec((D, tf), lambda i, f: (0, f)),
                  pl.BlockSpec((tf, D), lambda i, f: (f, 0)),
                  pl.BlockSpec(memory_space=pl.ANY),
                  pl.BlockSpec((1, 1, D), lambda i, f: ((i * tm) // seq, 0, 0)),
                  pl.BlockSpec((1, D), lambda i, f: (0, 0))],
        out_specs=pl.BlockSpec(memory_space=pl.ANY),
        out_shape=jax.ShapeDtypeStruct((T, D), F32),
        scratch_shapes=[pltpu.VMEM((tm, D), F32),
                        pltpu.VMEM((2, rows, D), F32),
                        pltpu.SemaphoreType.DMA((2,)),
                        pltpu.SemaphoreType.DMA((tm // rows,))],
        compiler_params=pltpu.CompilerParams(dimension_semantics=("arbitrary", "arbitrary"),
                                             vmem_limit_bytes=FFN_VMEM_LIMIT),
        name="ffn",
    )(u2, w1, w2, h, gate, final_g.reshape(1, D))


def kernel(x, c, w_ada, b_ada, norm1_g, w_in, conv_w, pool_w, pool_scale, a_log, dt_bias,
           head_norm_g, w_up_pool, w_up_delta, w_out, norm2_g, w_ff_in, w_ff_out, final_g):
    B, S, D = x.shape
    T = B * S
    H = a_log.shape[0]
    KW = H * HEAD_DIM
    PW = pool_scale.shape[0]
    assert w_in.shape[1] == PW + 3 * KW + KW + 2 * H + 2 * D
    assert w_ada.shape[1] == N_MOD * D and S % CHUNK == 0 and 2 * H <= LANES

    mod = _ada(c, w_ada, b_ada)
    shift1, scale1, gate1, shift2, scale2, gate2 = (
        mod[:, i * D:(i + 1) * D].reshape(B, 1, D) for i in range(N_MOD))

    c_ab = PW + 4 * KW
    w_bf = w_in.astype(BF16)
    w_gates = w_bf[:, c_ab + 2 * H:]
    w_ab = jnp.pad(w_bf[:, c_ab:c_ab + 2 * H], ((0, 0), (0, LANES - 2 * H)))

    x2 = x.reshape(T, D)
    u = _norm_mod(x2, norm1_g, shift1, scale1, S)
    proj, ab, _ = _matmul(u, w_bf, BF16, n_cols=c_ab, name="inproj", narrow=w_ab)
    G, Cg, _ = pool_w.shape
    gates, _, (w_up_pool, w_up_delta, w_out, w_ff_in, pool_w) = _matmul(
        u, w_gates, BF16, name="inproj_gates",
        cast_along=(w_up_pool, w_up_delta, w_out, w_ff_in, pool_w.reshape(G * Cg, Cg)))

    yp = _pool(proj, pool_w.reshape(G, Cg, Cg), pool_scale, S)
    yd = _delta(proj, ab, conv_w, a_log, dt_bias, head_norm_g,
                batch=B, seq=S, n_heads=H, col0=PW)
    merged, w_ff_out = _merge(yp, yd, w_up_pool, w_up_delta, gates, cast_along=w_ff_out)
    h = _outproj(merged, w_out, x2, gate1, S)

    u2 = _norm_mod(h, norm2_g, shift2, scale2, S)
    out = _ffn(u2, w_ff_in, w_ff_out, h, gate2, final_g, S)
    return out.reshape(B, S, D)
```
